```python
import math
import jax, jax.numpy as jnp
from jax import lax
import numpy as np

D_MODEL = 1024
BATCH = 2
SEQ = 8192
DEPTH = 4
DEC_BATCH = 128
DEC_SEQ = 1
PAST_LEN = 2048
PAGE_SIZE = 128

HEAD_DIM = 64
N_MIX_HEADS = D_MODEL // HEAD_DIM
H_A = N_MIX_HEADS // 2
G_B = N_MIX_HEADS - H_A
G_C = N_MIX_HEADS // 2
H_D = N_MIX_HEADS - G_C
D_A = H_A * HEAD_DIM
D_B = G_B * HEAD_DIM
D_C = G_C * HEAD_DIM
D_D = H_D * HEAD_DIM
MOBA_BLOCK = 256
MOBA_TOPK = 3
MOBA_QCHUNK = 64
GMLP_CHUNK = 128
CONV_W = 3
FOX_QBLOCK = 128
D_FF = 2816
PLE_DIM = 256
N_EVEN = (DEPTH + 1) // 2
N_ODD = DEPTH // 2
RMS_EPS = 1e-6
NEG_INF = -1e30
FORGET_BIAS_INIT = 3.0
ATTN_SCALE = 1.0 / math.sqrt(HEAD_DIM)

kernel_name = 'moba_gmlp_conv_fox_macaron_decoder_step'


def rms_norm(x, g):
    xf = x.astype(jnp.float32)
    y = xf * lax.rsqrt(jnp.mean(xf * xf, axis=-1, keepdims=True) + RMS_EPS)
    return (y * g.astype(jnp.float32)).astype(x.dtype)


def half_ffn(x, g, wi, wo):
    a, b = jnp.split(rms_norm(x, g) @ wi, 2, axis=-1)
    return 0.5 * ((jax.nn.silu(a) * b) @ wo)


def ple_add(x, p, g, wg, wp):
    return x + jax.nn.sigmoid(rms_norm(x, g) @ wg) * (p.astype(x.dtype) @ wp)


def to_blocks(x, n):
    b, s = x.shape[:2]
    return jnp.moveaxis(x.reshape(b, s // n, n, *x.shape[2:]), 1, 0)


def from_blocks(x):
    nb, b, n = x.shape[:3]
    return jnp.moveaxis(x, 0, 1).reshape(b, nb * n, *x.shape[3:])


def gather_pages(cache, page_table, layer):
    g = cache[page_table, layer]
    return g.reshape(g.shape[0], g.shape[1] * g.shape[2], *g.shape[3:])


def even_in(h, w_in, q_g, k_g, v_g):
    lead = h.shape[:-1]
    q, k, v, u, vg = jnp.split(h @ w_in, [D_A, 2 * D_A, 3 * D_A, 3 * D_A + D_B], axis=-1)
    q = rms_norm(q.reshape(*lead, H_A, HEAD_DIM), q_g)
    k = rms_norm(k.reshape(*lead, H_A, HEAD_DIM), k_g)
    v = v.reshape(*lead, H_A, HEAD_DIM)
    u = jax.nn.gelu(u)
    vg = rms_norm(jax.nn.gelu(vg).reshape(*lead, G_B, HEAD_DIM), v_g)
    return q, k, v, u, vg


def moba_blocks(k, v):
    b, L = k.shape[:2]
    nb = max(-(-L // MOBA_BLOCK), MOBA_TOPK)
    pad = nb * MOBA_BLOCK - L

    def blk(t):
        t = jnp.pad(t, ((0, 0), (0, pad), (0, 0), (0, 0)))
        return t.reshape(b, nb, MOBA_BLOCK, H_A, HEAD_DIM).transpose(0, 3, 1, 2, 4)

    kb, vb = blk(k), blk(v)
    kmean = jnp.mean(kb.astype(jnp.float32), axis=3)
    return kb, vb, kmean


def moba_query(q, q_pos, kb, vb, kmean):
    b, nq = q.shape[:2]
    nb = kb.shape[2]
    own = q_pos // MOBA_BLOCK
    gate = jnp.einsum('bqhd,bhnd->bqhn', q.astype(jnp.float32), kmean)
    past_ok = jnp.arange(nb)[None, :] < own[:, None]
    gate = jnp.where(past_ok[None, :, None, :], gate, NEG_INF)
    _, sel = lax.top_k(gate, MOBA_TOPK)
    sel_ok = sel < own[None, :, None, None]
    bi = jnp.arange(b)[:, None, None, None]
    hi = jnp.arange(H_A)[None, None, :, None]
    k_sel = kb[bi, hi, sel]
    v_sel = vb[bi, hi, sel]
    k_own = kb[:, :, own]
    v_own = vb[:, :, own]
    n_sel = MOBA_TOPK * MOBA_BLOCK
    s_sel = jnp.einsum('bqhd,bqhknd->bqhkn', q, k_sel).astype(jnp.float32) * ATTN_SCALE
    s_sel = jnp.where(sel_ok[..., None], s_sel, NEG_INF).reshape(b, nq, H_A, n_sel)
    own_pos = own[:, None] * MOBA_BLOCK + jnp.arange(MOBA_BLOCK)[None, :]
    s_own = jnp.einsum('bqhd,bhqnd->bqhn', q, k_own).astype(jnp.float32) * ATTN_SCALE
    s_own = jnp.where((own_pos <= q_pos[:, None])[None, :, None, :], s_own, NEG_INF)
    p = jax.nn.softmax(jnp.concatenate([s_sel, s_own], axis=-1), axis=-1).astype(v_sel.dtype)
    p_sel = p[..., :n_sel].reshape(b, nq, H_A, MOBA_TOPK, MOBA_BLOCK)
    return (jnp.einsum('bqhkn,bqhknd->bqhd', p_sel, v_sel)
            + jnp.einsum('bqhn,bhqnd->bqhd', p[..., n_sel:], v_own))


def chunk_spatial_gate(u, vg, w_s, b_s):
    b, n = u.shape[:2]
    c = GMLP_CHUNK if n >= GMLP_CHUNK else n
    nc = -(-n // c)
    pad = nc * c - n
    up = jnp.pad(u, ((0, 0), (0, pad), (0, 0))).reshape(b, nc, c, D_B)
    vp = jnp.pad(vg, ((0, 0), (0, pad), (0, 0), (0, 0))).reshape(b, nc, c, G_B, HEAD_DIM)
    w = jnp.tril(w_s[:, :c, :c])
    mixed = jnp.einsum('gts,bcsgd->bctgd', w, vp) + b_s[:, :c].T[:, :, None]
    return (up * mixed.reshape(up.shape)).reshape(b, nc * c, D_B)[:, :n]


def odd_in(h, w_in, b_f, q_g, k_g):
    lead = h.shape[:-1]
    idx = [int(i) for i in np.cumsum([D_C, D_C, D_C, D_D, D_D, D_D])]
    gb, gc, hc, q, k, v, fl = jnp.split(h @ w_in, idx, axis=-1)
    pre = gc * hc
    q = rms_norm(q.reshape(*lead, H_D, HEAD_DIM), q_g)
    k = rms_norm(k.reshape(*lead, H_D, HEAD_DIM), k_g)
    v = v.reshape(*lead, H_D, HEAD_DIM)
    logf = jax.nn.log_sigmoid(fl.astype(jnp.float32) + b_f.astype(jnp.float32))
    return gb, pre, q, k, v, logf


def short_conv(pre, buf, w):
    n = pre.shape[1]
    full = jnp.concatenate([buf.astype(pre.dtype), pre], axis=1)
    out = w[0] * full[:, :n]
    for j in range(1, CONV_W):
        out = out + w[j] * full[:, j:j + n]
    return out, full[:, n:]


def fox_attend(q, q_pos, f_q, k, v, f_k):
    L = k.shape[1]
    s = jnp.einsum('bqhd,bkhd->bhqk', q, k).astype(jnp.float32) * ATTN_SCALE
    s = s + jnp.swapaxes(f_q, 1, 2)[..., None] - jnp.swapaxes(f_k, 1, 2)[:, :, None, :]
    causal = jnp.arange(L)[None, :] <= q_pos[:, None]
    s = jnp.where(causal[None, None], s, NEG_INF)
    p = jax.nn.softmax(s, axis=-1).astype(v.dtype)
    return jnp.einsum('bhqk,bkhd->bqhd', p, v)


def setup_inputs(seed: int = 0) -> dict:
    key = jax.random.key(seed)
    ks = iter(jax.random.split(key, 40))
    f32 = jnp.float32

    def nrm(shape, scale=1.0):
        return jax.random.normal(next(ks), shape, f32) * scale

    def gain(shape):
        return 1.0 + 0.05 * nrm(shape)

    n_pages = PAST_LEN // PAGE_SIZE
    n_used = DEC_BATCH * n_pages
    n_pool = n_used + max(1, n_used // 4)
    perm = jax.random.permutation(next(ks), n_pool)[:n_used]
    page_table = perm.reshape(DEC_BATCH, n_pages).astype(jnp.int32)
    d_in_even = 3 * D_A + 2 * D_B
    d_in_odd = 3 * D_C + 3 * D_D + H_D
    return {
        'x_prompt': nrm((BATCH, SEQ, D_MODEL)),
        'x_sample': nrm((DEC_BATCH, DEC_SEQ, D_MODEL)),
        'cache_a_k': nrm((n_pool, N_EVEN, PAGE_SIZE, H_A, HEAD_DIM)),
        'cache_a_v': nrm((n_pool, N_EVEN, PAGE_SIZE, H_A, HEAD_DIM)),
        'cache_d_k': nrm((n_pool, N_ODD, PAGE_SIZE, H_D, HEAD_DIM)),
        'cache_d_v': nrm((n_pool, N_ODD, PAGE_SIZE, H_D, HEAD_DIM)),
        'cache_d_logf': jax.nn.log_sigmoid(FORGET_BIAS_INIT + nrm((n_pool, N_ODD, PAGE_SIZE, H_D))),
        'state_c_conv': nrm((DEC_BATCH, N_ODD, CONV_W - 1, D_C)),
        'page_table': page_table,
        'p_prompt': nrm((DEPTH, BATCH, SEQ, PLE_DIM)),
        'p_sample': nrm((DEPTH, DEC_BATCH, DEC_SEQ, PLE_DIM)),
        'norm_ffn1': gain((DEPTH, D_MODEL)),
        'ffn1_wi': nrm((DEPTH, D_MODEL, 2 * D_FF), D_MODEL ** -0.5),
        'ffn1_wo': nrm((DEPTH, D_FF, D_MODEL), D_FF ** -0.5),
        'norm_mix': gain((DEPTH, D_MODEL)),
        'ev_w_in': nrm((N_EVEN, D_MODEL, d_in_even), D_MODEL ** -0.5),
        'a_q_norm': gain((N_EVEN, HEAD_DIM)),
        'a_k_norm': gain((N_EVEN, HEAD_DIM)),
        'b_v_norm': gain((N_EVEN, G_B, HEAD_DIM)),
        'b_w_s': nrm((N_EVEN, G_B, GMLP_CHUNK, GMLP_CHUNK), GMLP_CHUNK ** -0.5),
        'b_bias': 1.0 + 0.1 * nrm((N_EVEN, G_B, GMLP_CHUNK)),
        'ev_w_out': nrm((N_EVEN, D_A + D_B, D_MODEL), (D_A + D_B) ** -0.5),
        'od_w_in': nrm((N_ODD, D_MODEL, d_in_odd), D_MODEL ** -0.5),
        'od_b_f': FORGET_BIAS_INIT + 0.1 * nrm((N_ODD, H_D)),
        'c_conv_w': nrm((N_ODD, CONV_W, D_C), CONV_W ** -0.5),
        'd_q_norm': gain((N_ODD, HEAD_DIM)),
        'd_k_norm': gain((N_ODD, HEAD_DIM)),
        'od_w_out': nrm((N_ODD, D_C + D_D, D_MODEL), (D_C + D_D) ** -0.5),
        'norm_ffn2': gain((DEPTH, D_MODEL)),
        'ffn2_wi': nrm((DEPTH, D_MODEL, 2 * D_FF), D_MODEL ** -0.5),
        'ffn2_wo': nrm((DEPTH, D_FF, D_MODEL), D_FF ** -0.5),
        'norm_ple': gain((DEPTH, D_MODEL)),
        'ple_w_gate': nrm((DEPTH, D_MODEL, D_MODEL), D_MODEL ** -0.5),
        'ple_w_proj': nrm((DEPTH, PLE_DIM, D_MODEL), PLE_DIM ** -0.5),
    }


def reference(x_prompt, x_sample, cache_a_k, cache_a_v, cache_d_k, cache_d_v, cache_d_logf,
              state_c_conv, page_table, p_prompt, p_sample,
              norm_ffn1, ffn1_wi, ffn1_wo, norm_mix,
              ev_w_in, a_q_norm, a_k_norm, b_v_norm, b_w_s, b_bias, ev_w_out,
              od_w_in, od_b_f, c_conv_w, d_q_norm, d_k_norm, od_w_out,
              norm_ffn2, ffn2_wi, ffn2_wo, norm_ple, ple_w_gate, ple_w_proj):
    bp, s_len = x_prompt.shape[:2]
    bs, t_len = x_sample.shape[:2]
    past = page_table.shape[1] * cache_a_k.shape[2]
    pos_p = jnp.arange(s_len, dtype=jnp.int32)
    pos_s = past + jnp.arange(t_len, dtype=jnp.int32)

    ak_p, av_p, ak_s, av_s, bv_s = [], [], [], [], []
    cv_p, cv_s = [], []
    dk_p, dv_p, df_p, dk_s, dv_s, df_s = [], [], [], [], [], []

    xp, xs = x_prompt, x_sample
    for l in range(DEPTH):
        xp = xp + half_ffn(xp, norm_ffn1[l], ffn1_wi[l], ffn1_wo[l])
        xs = xs + half_ffn(xs, norm_ffn1[l], ffn1_wi[l], ffn1_wo[l])
        hp = rms_norm(xp, norm_mix[l])
        hs = rms_norm(xs, norm_mix[l])
        if l % 2 == 0:
            e = l // 2
            q, k, v, u, vg = even_in(hp, ev_w_in[e], a_q_norm[e], a_k_norm[e], b_v_norm[e])
            kb, vb, km = moba_blocks(k, v)
            att = from_blocks(lax.map(
                lambda a: moba_query(a[0], a[1], kb, vb, km),
                (to_blocks(q, MOBA_QCHUNK), pos_p.reshape(-1, MOBA_QCHUNK))))
            sg = chunk_spatial_gate(u, vg, b_w_s[e], b_bias[e])
            xp = xp + jnp.concatenate([att.reshape(bp, s_len, D_A), sg], axis=-1) @ ev_w_out[e]
            ak_p.append(k)
            av_p.append(v)
            q, k, v, u, vg = even_in(hs, ev_w_in[e], a_q_norm[e], a_k_norm[e], b_v_norm[e])
            k_all = jnp.concatenate([gather_pages(cache_a_k, page_table, e).astype(k.dtype), k], axis=1)
            v_all = jnp.concatenate([gather_pages(cache_a_v, page_table, e).astype(v.dtype), v], axis=1)
            att = moba_query(q, pos_s, *moba_blocks(k_all, v_all))
            sg = chunk_spatial_gate(u, vg, b_w_s[e], b_bias[e])
            xs = xs + jnp.concatenate([att.reshape(bs, t_len, D_A), sg], axis=-1) @ ev_w_out[e]
            ak_s.append(k)
            av_s.append(v)
            bv_s.append(vg.reshape(bs, t_len, D_B))
        else:
            o = l // 2
            gb, pre, q, k, v, logf = odd_in(hp, od_w_in[o], od_b_f[o], d_q_norm[o], d_k_norm[o])
            conv, buf = short_conv(pre, jnp.zeros((bp, CONV_W - 1, D_C), pre.dtype), c_conv_w[o])
            fcum = jnp.cumsum(logf, axis=1)
            att = from_blocks(lax.map(
                lambda a: fox_attend(a[0], a[1], a[2], k, v, fcum),
                (to_blocks(q, FOX_QBLOCK), pos_p.reshape(-1, FOX_QBLOCK), to_blocks(fcum, FOX_QBLOCK))))
            xp = xp + jnp.concatenate([gb * conv, att.reshape(bp, s_len, D_D)], axis=-1) @ od_w_out[o]
            cv_p.append(buf)
            dk_p.append(k)
            dv_p.append(v)
            df_p.append(logf)
            gb, pre, q, k, v, logf = odd_in(hs, od_w_in[o], od_b_f[o], d_q_norm[o], d_k_norm[o])
            conv, buf = short_conv(pre, state_c_conv[:, o], c_conv_w[o])
            k_all = jnp.concatenate([gather_pages(cache_d_k, page_table, o).astype(k.dtype), k], axis=1)
            v_all = jnp.concatenate([gather_pages(cache_d_v, page_table, o).astype(v.dtype), v], axis=1)
            f_all = jnp.concatenate([gather_pages(cache_d_logf, page_table, o).astype(jnp.float32), logf], axis=1)
            fcum = jnp.cumsum(f_all, axis=1)
            att = fox_attend(q, pos_s, fcum[:, past:], k_all, v_all, fcum)
            xs = xs + jnp.concatenate([gb * conv, att.reshape(bs, t_len, D_D)], axis=-1) @ od_w_out[o]
            cv_s.append(buf)
            dk_s.append(k)
            dv_s.append(v)
            df_s.append(logf)
        xp = xp + half_ffn(xp, norm_ffn2[l], ffn2_wi[l], ffn2_wo[l])
        xs = xs + half_ffn(xs, norm_ffn2[l], ffn2_wi[l], ffn2_wo[l])
        xp = ple_add(xp, p_prompt[l], norm_ple[l], ple_w_gate[l], ple_w_proj[l])
        xs = ple_add(xs, p_sample[l], norm_ple[l], ple_w_gate[l], ple_w_proj[l])

    st = lambda rows: jnp.stack(rows, axis=1)
    return (xp, xs,
            st(ak_p), st(av_p), st(ak_s), st(av_s), st(bv_s),
            st(cv_p), st(cv_s),
            st(dk_p), st(dv_p), st(df_p), st(dk_s), st(dv_s), st(df_s))
```

```python
import functools
import math

import numpy as np
import jax
import jax.numpy as jnp
from jax import lax
from jax.experimental import pallas as pl
from jax.experimental.pallas import tpu as pltpu

F32 = jnp.float32
BF16 = jnp.bfloat16

HEAD_DIM = 64
MOBA_BLOCK = 256
MOBA_TOPK = 3
GMLP_CHUNK = 128
CONV_W = 3
RMS_EPS = 1e-6
NEG_INF = -1e30
ATTN_SCALE = 1.0 / math.sqrt(HEAD_DIM)

LANES = 128
SUBLANES = 8
HEADS_PER_TILE = LANES // HEAD_DIM
DECAY_LANE0 = 96
DECAY_TERMS = 3
VMEM_LIMIT = 56 * 1024 * 1024
TOKEN_TILES = (512, 256, 128, 64, 32, 16, 8)
HIGHEST = lax.Precision.HIGHEST


def _params(*sem):
    return pltpu.CompilerParams(dimension_semantics=sem, vmem_limit_bytes=VMEM_LIMIT)


def _pick(n, options):
    for t in options:
        if n % t == 0:
            return t
    raise ValueError(f"no tile in {options} divides {n}")


def _rms(x, g):
    return x * lax.rsqrt(jnp.mean(x * x, axis=-1, keepdims=True) + RMS_EPS) * g


def _dot(a, b):
    return jnp.dot(a, b, preferred_element_type=F32)


def _dot_t(a, b, precision=None):
    return lax.dot_general(a, b, (((1,), (1,)), ((), ())), precision=precision,
                           preferred_element_type=F32)


def _split3(x):
    hi = x.astype(BF16)
    r = x - hi.astype(F32)
    mid = r.astype(BF16)
    lo = (r - mid.astype(F32)).astype(BF16)
    return hi, mid, lo


def _log_sigmoid(x):
    return jnp.minimum(x, 0.0) - jnp.log1p(jnp.exp(-jnp.abs(x)))


def _head_norm(t, e, gain):
    ss = _dot((t * t).astype(BF16), e)
    return t * lax.rsqrt(ss * (1.0 / HEAD_DIM) + RMS_EPS) * gain


def _ffn_body(x_ref, g_ref, wa_ref, wb_ref, wo_ref, o_ref, hn_ref, acc_ref):
    j = pl.program_id(1)

    @pl.when(j == 0)
    def _():
        hn_ref[...] = _rms(x_ref[...], g_ref[...]).astype(BF16)
        acc_ref[...] = jnp.zeros_like(acc_ref)

    hn = hn_ref[...]
    a = _dot(hn, wa_ref[...])
    b = _dot(hn, wb_ref[...])
    act = (jax.nn.silu(a) * b).astype(BF16)
    acc_ref[...] += _dot(act, wo_ref[...])

    @pl.when(j == pl.num_programs(1) - 1)
    def _():
        o_ref[...] = x_ref[...] + 0.5 * acc_ref[...]


def _ffn_half(x, g, wi, wo, layer):
    t, d = x.shape
    f = wo.shape[1]
    tm = _pick(t, TOKEN_TILES)
    tf = _pick(f, (1408, 512, 256, 128))
    nf = f // tf
    return pl.pallas_call(
        _ffn_body,
        grid=(t // tm, nf),
        in_specs=[
            pl.BlockSpec((tm, d), lambda i, j: (i, 0)),
            pl.BlockSpec((None, 1, d), lambda i, j: (layer, 0, 0)),
            pl.BlockSpec((None, d, tf), lambda i, j: (layer, 0, j)),
            pl.BlockSpec((None, d, tf), lambda i, j: (layer, 0, j + nf)),
            pl.BlockSpec((None, tf, d), lambda i, j: (layer, j, 0)),
        ],
        out_specs=pl.BlockSpec((tm, d), lambda i, j: (i, 0)),
        out_shape=jax.ShapeDtypeStruct((t, d), F32),
        scratch_shapes=[pltpu.VMEM((tm, d), BF16), pltpu.VMEM((tm, d), F32)],
        compiler_params=_params("parallel", "arbitrary"),
        name="ffn_half",
    )(x, g, wi, wi, wo)


def _ple_body(x_ref, p_ref, g_ref, wg_ref, wp_ref, o_ref):
    x = x_ref[...]
    h = _rms(x, g_ref[...]).astype(BF16)
    gate = jax.nn.sigmoid(_dot(h, wg_ref[...]))
    proj = _dot(p_ref[...].astype(BF16), wp_ref[...])
    o_ref[...] = x + gate * proj


def _ple_add(x, p, g, wg, wp, layer):
    t, d = x.shape
    pdim = p.shape[-1]
    tm = _pick(t, TOKEN_TILES)
    return pl.pallas_call(
        _ple_body,
        grid=(t // tm,),
        in_specs=[
            pl.BlockSpec((tm, d), lambda i: (i, 0)),
            pl.BlockSpec((None, tm, pdim), lambda i: (layer, i, 0)),
            pl.BlockSpec((None, 1, d), lambda i: (layer, 0, 0)),
            pl.BlockSpec((None, d, d), lambda i: (layer, 0, 0)),
            pl.BlockSpec((None, pdim, d), lambda i: (layer, 0, 0)),
        ],
        out_specs=pl.BlockSpec((tm, d), lambda i: (i, 0)),
        out_shape=jax.ShapeDtypeStruct((t, d), F32),
        compiler_params=_params("parallel"),
        name="ple_add",
    )(x, p, g, wg, wp)


def _mix_out_body(x_ref, a_ref, b_ref, w_ref, o_ref):
    half = a_ref.shape[-1]
    o_ref[...] = (x_ref[...]
                  + _dot(a_ref[...].astype(BF16), w_ref[0:half, :])
                  + _dot(b_ref[...].astype(BF16), w_ref[half:2 * half, :]))


def _mix_out_gate_body(x_ref, a_ref, u_ref, vg_ref, wv_ref, bv_ref, w_ref, o_ref):
    half = a_ref.shape[-1]
    sg = u_ref[...] * (wv_ref[...] * vg_ref[...] + bv_ref[...])
    o_ref[...] = (x_ref[...]
                  + _dot(a_ref[...].astype(BF16), w_ref[0:half, :])
                  + _dot(sg.astype(BF16), w_ref[half:2 * half, :]))


def _mix_out(x, parts, w, layer, gate_vecs=None):
    t, d = x.shape
    half = parts[0].shape[-1]
    tm = _pick(t, TOKEN_TILES)
    row = lambda width: pl.BlockSpec((tm, width), lambda i: (i, 0))
    vec = pl.BlockSpec((1, half), lambda i: (0, 0))
    w_spec = pl.BlockSpec((None, 2 * half, d), lambda i: (layer, 0, 0))
    if gate_vecs is None:
        body, in_specs, args = _mix_out_body, [row(d), row(half), row(half), w_spec], (x, *parts, w)
    else:
        body = _mix_out_gate_body
        in_specs = [row(d), row(half), row(half), row(half), vec, vec, w_spec]
        args = (x, *parts, *gate_vecs, w)
    return pl.pallas_call(
        body,
        grid=(t // tm,),
        in_specs=in_specs,
        out_specs=row(d),
        out_shape=jax.ShapeDtypeStruct((t, d), F32),
        compiler_params=_params("parallel"),
        name="mix_out",
    )(*args)


def _gmlp_body(u_ref, vg_ref, ws_ref, bias_ref, o_ref):
    tm = u_ref.shape[0]
    c = GMLP_CHUNK
    r_i = lax.broadcasted_iota(jnp.int32, (c, c), 0)
    c_i = lax.broadcasted_iota(jnp.int32, (c, c), 1)
    tril = c_i <= r_i
    first_head = c_i < HEAD_DIM
    n_groups = ws_ref.shape[0]
    w = [jnp.where(tril, ws_ref[g], 0.0).astype(BF16) for g in range(n_groups)]
    bias = bias_ref[...]
    for ci in range(tm // c):
        rows = pl.ds(ci * c, c)
        vg = vg_ref[rows, :].astype(BF16)
        tiles = []
        for t in range(n_groups // HEADS_PER_TILE):
            vt = vg[:, t * LANES:(t + 1) * LANES]
            tiles.append(jnp.where(first_head, _dot(w[2 * t], vt), _dot(w[2 * t + 1], vt)))
        mixed = jnp.concatenate(tiles, axis=1) + bias
        o_ref[rows, :] = u_ref[rows, :] * mixed


def _gmlp_gate(u, vg, ws, bias, layer):
    t, width = u.shape
    tm = _pick(t, TOKEN_TILES)
    g, c = ws.shape[1], ws.shape[2]
    row = pl.BlockSpec((tm, width), lambda i: (i, 0))
    return pl.pallas_call(
        _gmlp_body,
        grid=(t // tm,),
        in_specs=[row, row,
                  pl.BlockSpec((None, g, c, c), lambda i: (layer, 0, 0, 0)),
                  pl.BlockSpec((None, c, width), lambda i: (layer, 0, 0))],
        out_specs=row,
        out_shape=jax.ShapeDtypeStruct((t, width), F32),
        compiler_params=_params("parallel"),
        name="gmlp_gate",
    )(u, vg, ws, bias)


def _block_onehot(tile_rows, first_pos, nb):
    r = lax.broadcasted_iota(jnp.int32, (tile_rows, LANES), 0)
    c = lax.broadcasted_iota(jnp.int32, (tile_rows, LANES), 1)
    blk = lax.div(first_pos + r, MOBA_BLOCK)
    return jnp.where((c == blk) & (c < nb), 1.0, 0.0)


def _even_in_body(x_ref, g_ref, w_ref, qg_ref, kg_ref, vn_ref, e_ref, *outs, prompt, seq_len):
    q_ref, k_ref, v_ref, u_ref, vg_ref = outs[:5]
    half = q_ref.shape[-1]
    h = _rms(x_ref[...], g_ref[...]).astype(BF16)
    z = _dot(h, w_ref[...])
    e = e_ref[...]
    q = _head_norm(z[:, 0:half], e, qg_ref[...])
    k = _head_norm(z[:, half:2 * half], e, kg_ref[...])
    v = z[:, 2 * half:3 * half]
    q_ref[...] = q
    k_ref[...] = k
    v_ref[...] = v
    u_ref[...] = jax.nn.gelu(z[:, 3 * half:4 * half])
    vg_ref[...] = _head_norm(jax.nn.gelu(z[:, 4 * half:5 * half]), e, vn_ref[...])
    if prompt:
        kaug_ref, vb_ref, km_ref = outs[5:]
        tm = x_ref.shape[0]
        first_pos = lax.rem(pl.program_id(0) * tm, seq_len)
        onehot = _block_onehot(tm, first_pos, seq_len // MOBA_BLOCK).astype(BF16)
        kb = k.astype(BF16)
        for t in range(half // LANES):
            kaug_ref[:, 2 * t * LANES:(2 * t + 1) * LANES] = kb[:, t * LANES:(t + 1) * LANES]
            kaug_ref[:, (2 * t + 1) * LANES:(2 * t + 2) * LANES] = onehot
        vb_ref[...] = v.astype(BF16)
        for b in range(tm // MOBA_BLOCK):
            km_ref[b:b + 1, :] = jnp.mean(k[b * MOBA_BLOCK:(b + 1) * MOBA_BLOCK, :], axis=0, keepdims=True)


def _even_in(x, g, w, qg, kg, vn, e, layer, ev, prompt, seq_len):
    t, d = x.shape
    n_in = w.shape[-1]
    half = n_in // 5
    tm = _pick(t, (512, 256) if prompt else TOKEN_TILES)
    row = lambda width: pl.BlockSpec((tm, width), lambda i: (i, 0))
    vec = lambda arr: pl.BlockSpec((None, 1, arr.shape[-1]), lambda i: (ev, 0, 0))
    out_specs = [row(half)] * 5
    out_shape = [jax.ShapeDtypeStruct((t, half), F32)] * 5
    if prompt:
        nkm = tm // MOBA_BLOCK
        out_specs += [row(2 * half), row(half), pl.BlockSpec((None, nkm, half), lambda i: (i, 0, 0))]
        out_shape += [jax.ShapeDtypeStruct((t, 2 * half), BF16), jax.ShapeDtypeStruct((t, half), BF16),
                      jax.ShapeDtypeStruct((t // tm, nkm, half), F32)]
    return pl.pallas_call(
        functools.partial(_even_in_body, prompt=prompt, seq_len=seq_len),
        grid=(t // tm,),
        in_specs=[row(d),
                  pl.BlockSpec((None, 1, d), lambda i: (layer, 0, 0)),
                  pl.BlockSpec((None, d, n_in), lambda i: (ev, 0, 0)),
                  vec(qg), vec(kg), vec(vn),
                  pl.BlockSpec(e.shape, lambda i: (0, 0))],
        out_specs=out_specs,
        out_shape=out_shape,
        compiler_params=_params("parallel"),
        name="even_in",
    )(x, g, w, qg, kg, vn, e)


def _odd_in_prompt_body(x_ref, g_ref, w_ref, wf_ref, bf_ref, qg_ref, kg_ref, cw_ref, e_ref, p_ref,
                        yc_ref, q_ref, k_ref, v_ref, lf_ref, kaug_ref, vb_ref, tail_ref,
                        pre_ref, carry_ref, *, seq_len):
    tm = x_ref.shape[0]
    half = q_ref.shape[-1]
    n_heads = lf_ref.shape[-1]
    tiles_per_seq = seq_len // tm
    s_idx = lax.rem(pl.program_id(0), tiles_per_seq)

    h = _rms(x_ref[...], g_ref[...]).astype(BF16)
    z = _dot(h, w_ref[...])
    e = e_ref[...]
    gb = z[:, 0:half]
    pre = z[:, half:2 * half] * z[:, 2 * half:3 * half]
    q_ref[...] = _head_norm(z[:, 3 * half:4 * half], e, qg_ref[...])
    k = _head_norm(z[:, 4 * half:5 * half], e, kg_ref[...])
    v = z[:, 5 * half:6 * half]
    k_ref[...] = k
    v_ref[...] = v
    vb_ref[...] = v.astype(BF16)

    @pl.when(s_idx == 0)
    def _():
        pre_ref[0:SUBLANES, :] = jnp.zeros((SUBLANES, half), F32)
        carry_ref[...] = jnp.zeros_like(carry_ref)

    @pl.when(s_idx != 0)
    def _():
        pre_ref[0:SUBLANES, :] = pre_ref[tm:tm + SUBLANES, :]

    pre_ref[SUBLANES:SUBLANES + tm, :] = pre
    conv = (cw_ref[0:1, :] * pre_ref[SUBLANES - 2:SUBLANES - 2 + tm, :]
            + cw_ref[1:2, :] * pre_ref[SUBLANES - 1:SUBLANES - 1 + tm, :]
            + cw_ref[2:3, :] * pre)
    yc_ref[...] = gb * conv
    tail_ref[...] = pre[tm - SUBLANES:tm, :]

    lane = lax.broadcasted_iota(jnp.int32, (tm, LANES), 1)
    logf = jnp.where(lane < n_heads, _log_sigmoid(_dot(h, wf_ref[...]) + bf_ref[...]), 0.0)
    lf_ref[...] = logf[:, 0:n_heads]
    r_i = lax.broadcasted_iota(jnp.int32, (tm, tm), 0)
    c_i = lax.broadcasted_iota(jnp.int32, (tm, tm), 1)
    tril = jnp.where(c_i <= r_i, 1.0, 0.0).astype(BF16)
    hi, mid, lo = _split3(logf)
    cum = _dot(tril, hi) + _dot(tril, mid) + _dot(tril, lo) + carry_ref[...]
    carry_ref[...] = cum[tm - 1:tm, :]

    dec = _dot(jnp.concatenate(_split3(-cum), axis=1), p_ref[...])
    onehot = _block_onehot(tm, s_idx * tm, seq_len // MOBA_BLOCK)
    kb = k.astype(BF16)
    for t in range(half // LANES):
        kaug_ref[:, 2 * t * LANES:(2 * t + 1) * LANES] = kb[:, t * LANES:(t + 1) * LANES]
        kaug_ref[:, (2 * t + 1) * LANES:(2 * t + 2) * LANES] = (
            dec[:, t * LANES:(t + 1) * LANES] + onehot).astype(BF16)


def _odd_in_prompt(x, g, w, wf, bf, qg, kg, cw, e, place, layer, od, seq_len, n_heads):
    t, d = x.shape
    half = qg.shape[-1]
    tm = _pick(seq_len, (512, 256))
    row = lambda width: pl.BlockSpec((tm, width), lambda i: (i, 0))
    vec = lambda arr: pl.BlockSpec((None,) + arr.shape[1:], lambda i: (od, 0, 0))
    full = lambda arr: pl.BlockSpec(arr.shape, lambda i: (0, 0))
    tiles_per_seq = seq_len // tm
    return pl.pallas_call(
        functools.partial(_odd_in_prompt_body, seq_len=seq_len),
        grid=(t // tm,),
        in_specs=[row(d),
                  pl.BlockSpec((None, 1, d), lambda i: (layer, 0, 0)),
                  vec(w), vec(wf), vec(bf), vec(qg), vec(kg), vec(cw), full(e), full(place)],
        out_specs=[row(half), row(half), row(half), row(half), row(n_heads), row(2 * half), row(half),
                   pl.BlockSpec((None, SUBLANES, half), lambda i: (i // tiles_per_seq, 0, 0))],
        out_shape=[jax.ShapeDtypeStruct((t, half), F32)] * 4
        + [jax.ShapeDtypeStruct((t, n_heads), F32),
           jax.ShapeDtypeStruct((t, 2 * half), BF16), jax.ShapeDtypeStruct((t, half), BF16),
           jax.ShapeDtypeStruct((t // seq_len, SUBLANES, half), F32)],
        scratch_shapes=[pltpu.VMEM((tm + 2 * SUBLANES, half), F32), pltpu.VMEM((1, LANES), F32)],
        compiler_params=_params("arbitrary"),
        name="odd_in_prompt",
    )(x, g, w, wf, bf, qg, kg, cw, e, place)


def _odd_in_sample_body(x_ref, g_ref, w_ref, wf_ref, bf_ref, qg_ref, kg_ref, cw_ref, e_ref,
                        b0_ref, b1_ref, yc_ref, q_ref, k_ref, v_ref, lf_ref, pre_ref):
    half = q_ref.shape[-1]
    h = _rms(x_ref[...], g_ref[...]).astype(BF16)
    z = _dot(h, w_ref[...])
    e = e_ref[...]
    pre = z[:, half:2 * half] * z[:, 2 * half:3 * half]
    conv = cw_ref[0:1, :] * b0_ref[...] + cw_ref[1:2, :] * b1_ref[...] + cw_ref[2:3, :] * pre
    yc_ref[...] = z[:, 0:half] * conv
    pre_ref[...] = pre
    q_ref[...] = _head_norm(z[:, 3 * half:4 * half], e, qg_ref[...])
    k_ref[...] = _head_norm(z[:, 4 * half:5 * half], e, kg_ref[...])
    v_ref[...] = z[:, 5 * half:6 * half]
    lf_ref[...] = _log_sigmoid(_dot(h, wf_ref[...]) + bf_ref[...])


def _odd_in_sample(x, g, w, wf, bf, qg, kg, cw, e, b0, b1, layer, od):
    t, d = x.shape
    half = qg.shape[-1]
    tm = _pick(t, TOKEN_TILES)
    row = lambda width: pl.BlockSpec((tm, width), lambda i: (i, 0))
    vec = lambda arr: pl.BlockSpec((None,) + arr.shape[1:], lambda i: (od, 0, 0))
    return pl.pallas_call(
        _odd_in_sample_body,
        grid=(t // tm,),
        in_specs=[row(d),
                  pl.BlockSpec((None, 1, d), lambda i: (layer, 0, 0)),
                  vec(w), vec(wf), vec(bf), vec(qg), vec(kg), vec(cw),
                  pl.BlockSpec(e.shape, lambda i: (0, 0)), row(half), row(half)],
        out_specs=[row(half)] * 4 + [row(LANES), row(half)],
        out_shape=[jax.ShapeDtypeStruct((t, half), F32)] * 4
        + [jax.ShapeDtypeStruct((t, LANES), F32), jax.ShapeDtypeStruct((t, half), F32)],
        compiler_params=_params("parallel"),
        name="odd_in_sample",
    )(x, g, w, wf, bf, qg, kg, cw, e, b0, b1)


def _flash_body(q_ref, kaug_ref, v_ref, *rest, moba, nb, tk):
    if moba:
        km_ref, o_ref, qa_ref = rest
    else:
        o_ref, qa_ref = rest
    tq = MOBA_BLOCK
    qi = pl.program_id(2)
    q32 = q_ref[...]
    lane = lax.broadcasted_iota(jnp.int32, (tq, LANES), 1)
    is_blk = lane < nb
    past = lane < qi
    lane_f = lane.astype(F32)
    own_rows = pl.ds(pl.multiple_of(qi * tq, tq), tq)
    k_own = kaug_ref[own_rows, :]
    v_own = v_ref[own_rows, :]
    r_i = lax.broadcasted_iota(jnp.int32, (tq, tq), 0)
    c_i = lax.broadcasted_iota(jnp.int32, (tq, tq), 1)
    causal = c_i <= r_i

    init = []
    for hh in range(HEADS_PER_TILE):
        hmask = (lane >= hh * HEAD_DIM) & (lane < (hh + 1) * HEAD_DIM)
        qh = jnp.where(hmask, q32, 0.0)
        if moba:
            gate = jnp.where(past, _dot_t(qh, km_ref[...], precision=HIGHEST), NEG_INF)
            bias = jnp.full((tq, LANES), NEG_INF, F32)
            for _ in range(MOBA_TOPK):
                top = jnp.max(gate, axis=-1, keepdims=True)
                idx = jnp.min(jnp.where(gate == top, lane_f, float(LANES)), axis=-1, keepdims=True)
                pick = lane_f == idx
                bias = jnp.where(pick, jnp.where(past, 0.0, NEG_INF), bias)
                gate = jnp.where(pick, -jnp.inf, gate)
            aux_own = jnp.zeros((tq, LANES), F32)
        else:
            bias = jnp.where(past, 0.0, NEG_INF)
            d0 = DECAY_LANE0 + DECAY_TERMS * hh
            aux_own = jnp.where((lane >= d0) & (lane < d0 + DECAY_TERMS), 1.0, 0.0)
        aux = jnp.where(is_blk, bias, aux_own)
        qb = (qh * ATTN_SCALE).astype(BF16)
        qa_ref[hh] = jnp.concatenate([qb, aux.astype(BF16)], axis=1)
        s = _dot_t(jnp.concatenate([qb, aux_own.astype(BF16)], axis=1), k_own)
        s = jnp.where(causal, s, NEG_INF)
        m = jnp.max(s, axis=-1, keepdims=True)
        p = jnp.exp(s - m)
        init.append((m, jnp.sum(p, axis=-1, keepdims=True), _dot(p.astype(BF16), v_own)))

    def body(g, carry):
        rows = pl.ds(pl.multiple_of(g * tk, tk), tk)
        k_g = kaug_ref[rows, :]
        v_g = v_ref[rows, :]
        new = []
        for hh in range(HEADS_PER_TILE):
            m, l, acc = carry[hh]
            s = _dot_t(qa_ref[hh], k_g)
            m_new = jnp.maximum(m, jnp.max(s, axis=-1, keepdims=True))
            alpha = jnp.exp(m - m_new)
            p = jnp.exp(s - m_new)
            new.append((m_new, alpha * l + jnp.sum(p, axis=-1, keepdims=True),
                        alpha * acc + _dot(p.astype(BF16), v_g)))
        return tuple(new)

    blocks_per_step = tk // tq
    n_steps = lax.div(qi + (blocks_per_step - 1), blocks_per_step)
    (_, l0, a0), (_, l1, a1) = lax.fori_loop(0, n_steps, body, tuple(init))
    o_ref[...] = jnp.where(lane < HEAD_DIM, a0 * (1.0 / l0), a1 * (1.0 / l1))


def _flash(q, kaug, vb, kmean, batch, seq_len):
    t, width = q.shape
    tq = MOBA_BLOCK
    nq = seq_len // tq
    nb = seq_len // MOBA_BLOCK
    assert nb <= DECAY_LANE0 and DECAY_LANE0 + HEADS_PER_TILE * DECAY_TERMS <= LANES
    tk = _pick(seq_len, (1024, 512, 256))
    moba = kmean is not None
    in_specs = [pl.BlockSpec((tq, LANES), lambda b, hp, i: (b * nq + i, hp)),
                pl.BlockSpec((seq_len, 2 * LANES), lambda b, hp, i: (b, hp)),
                pl.BlockSpec((seq_len, LANES), lambda b, hp, i: (b, hp))]
    args = [q, kaug, vb]
    if moba:
        in_specs.append(pl.BlockSpec((LANES, LANES), lambda b, hp, i: (b, hp)))
        args.append(kmean)
    return pl.pallas_call(
        functools.partial(_flash_body, moba=moba, nb=nb, tk=tk),
        grid=(batch, width // LANES, nq),
        in_specs=in_specs,
        out_specs=pl.BlockSpec((tq, LANES), lambda b, hp, i: (b * nq + i, hp)),
        out_shape=jax.ShapeDtypeStruct((t, width), F32),
        scratch_shapes=[pltpu.VMEM((HEADS_PER_TILE, tq, 2 * LANES), BF16)],
        compiler_params=_params("parallel", "parallel", "arbitrary"),
        name="moba_attn" if moba else "fox_attn",
    )(*args)


def _paged_body(pt_ref, q_ref, kn_ref, vn_ref, *rest, moba, n_pages, n_heads):
    del pt_ref
    if moba:
        k_refs, v_refs, (o_ref,) = rest[:n_pages], rest[n_pages:2 * n_pages], rest[2 * n_pages:]
    else:
        lfn_ref = rest[0]
        k_refs, v_refs = rest[1:1 + n_pages], rest[1 + n_pages:1 + 2 * n_pages]
        f_refs, (o_ref,) = rest[1 + 2 * n_pages:1 + 3 * n_pages], rest[1 + 3 * n_pages:]
    page = k_refs[0].shape[1]
    width = q_ref.shape[-1]
    assert n_heads == SUBLANES
    q = q_ref[...]
    h_i = lax.broadcasted_iota(jnp.int32, (n_heads, width), 0)
    w_i = lax.broadcasted_iota(jnp.int32, (n_heads, width), 1)
    own = lax.div(w_i, HEAD_DIM) == h_i
    q8 = jnp.where(own, q, 0.0)
    h2_i = lax.broadcasted_iota(jnp.int32, (2 * n_heads, width), 0)
    w2_i = lax.broadcasted_iota(jnp.int32, (2 * n_heads, width), 1)
    qs = jnp.where(lax.div(w2_i, HEAD_DIM) == lax.rem(h2_i, n_heads), q, 0.0) * ATTN_SCALE
    qs_hi = qs.astype(BF16).astype(F32)
    q16 = jnp.where(h2_i < n_heads, qs_hi, qs - qs_hi).astype(BF16)
    s = []
    for p in range(n_pages):
        s16 = _dot(q16, k_refs[p][...].astype(BF16))
        s.append(s16[0:n_heads, :] + s16[n_heads:2 * n_heads, :])

    if moba:
        pages_per_blk = MOBA_BLOCK // page
        n_blk = n_pages // pages_per_blk
        assert MOBA_TOPK <= n_blk <= LANES
        lane = lax.broadcasted_iota(jnp.int32, (n_heads, LANES), 1)
        lane_f = lane.astype(F32)
        gate = jnp.full((n_heads, LANES), NEG_INF, F32)
        for b in range(n_blk):
            tot = sum(jnp.sum(s[b * pages_per_blk + j], axis=-1, keepdims=True) for j in range(pages_per_blk))
            gate = jnp.where(lane == b, tot, gate)
        chosen = jnp.zeros((n_heads, LANES), F32)
        for _ in range(MOBA_TOPK):
            top = jnp.max(gate, axis=-1, keepdims=True)
            idx = jnp.min(jnp.where(gate == top, lane_f, float(LANES)), axis=-1, keepdims=True)
            pick = lane_f == idx
            chosen = jnp.where(pick, 1.0, chosen)
            gate = jnp.where(pick, -jnp.inf, gate)
        for p in range(n_pages):
            on = jnp.max(jnp.where(lane == p // pages_per_blk, chosen, 0.0), axis=-1, keepdims=True)
            s[p] = jnp.where(on > 0.0, s[p], NEG_INF)
    else:
        r_i = lax.broadcasted_iota(jnp.int32, (page, page), 0)
        c_i = lax.broadcasted_iota(jnp.int32, (page, page), 1)
        after_in_page = jnp.where(r_i > c_i, 1.0, 0.0).astype(BF16)
        after = lfn_ref[...]
        for p in reversed(range(n_pages)):
            f = f_refs[p][...]
            hi, mid, lo = _split3(f)
            within = _dot(hi, after_in_page) + _dot(mid, after_in_page) + _dot(lo, after_in_page)
            s[p] = s[p] + (within + after)
            after = after + jnp.sum(f, axis=-1, keepdims=True)

    s_self = jnp.sum(q8 * kn_ref[...], axis=-1, keepdims=True) * ATTN_SCALE
    m = s_self
    for p in range(n_pages):
        m = jnp.maximum(m, jnp.max(s[p], axis=-1, keepdims=True))
    p_self = jnp.exp(s_self - m)
    l = p_self
    acc = p_self * vn_ref[...]
    for p in range(n_pages):
        e = jnp.exp(s[p] - m)
        l = l + jnp.sum(e, axis=-1, keepdims=True)
        acc = acc + _dot_t(e.astype(BF16), v_refs[p][...].astype(BF16))
    o_ref[...] = jnp.sum(jnp.where(own, acc * (1.0 / l), 0.0), axis=0, keepdims=True)


def _paged_attn(page_table, q, kn, vn, cache_k, cache_v, layer_idx, lf_new=None, cache_f=None):
    n, _, width = q.shape
    n_pages = page_table.shape[1]
    n_heads = width // HEAD_DIM
    moba = cache_f is None
    one = pl.BlockSpec((None, 1, width), lambda s, pt: (s, 0, 0))

    def paged(arr, p):
        return pl.BlockSpec((None, None) + arr.shape[2:], lambda s, pt: (pt[s, p], layer_idx, 0, 0))

    in_specs = [one, one, one]
    args = [q, kn, vn]
    if not moba:
        in_specs.append(pl.BlockSpec((None, n_heads, 1), lambda s, pt: (s, 0, 0)))
        args.append(lf_new)
    in_specs += [paged(cache_k, p) for p in range(n_pages)] + [paged(cache_v, p) for p in range(n_pages)]
    args += [cache_k] * n_pages + [cache_v] * n_pages
    if not moba:
        in_specs += [paged(cache_f, p) for p in range(n_pages)]
        args += [cache_f] * n_pages
    return pl.pallas_call(
        functools.partial(_paged_body, moba=moba, n_pages=n_pages, n_heads=n_heads),
        grid_spec=pltpu.PrefetchScalarGridSpec(
            num_scalar_prefetch=1,
            grid=(n,),
            in_specs=in_specs,
            out_specs=pl.BlockSpec((None, 1, width), lambda s, pt: (s, 0, 0)),
        ),
        out_shape=jax.ShapeDtypeStruct((n, 1, width), F32),
        compiler_params=_params("arbitrary"),
        name="moba_paged" if moba else "fox_paged",
    )(page_table, *args)


def _head_ones(width):
    i = np.arange(width) // HEAD_DIM
    return jnp.asarray(i[:, None] == i[None, :], dtype=BF16)


def _decay_placement(n_heads):
    place = np.zeros((DECAY_TERMS * LANES, n_heads // HEADS_PER_TILE * LANES), np.float32)
    for c in range(DECAY_TERMS):
        for h in range(n_heads):
            col = (h // HEADS_PER_TILE) * LANES + DECAY_LANE0 + DECAY_TERMS * (h % HEADS_PER_TILE) + c
            place[c * LANES + h, col] = 1.0
    return jnp.asarray(place, dtype=BF16)


def kernel(x_prompt, x_sample, cache_a_k, cache_a_v, cache_d_k, cache_d_v, cache_d_logf, state_c_conv, page_table, p_prompt, p_sample, norm_ffn1, ffn1_wi, ffn1_wo, norm_mix, ev_w_in, a_q_norm, a_k_norm, b_v_norm, b_w_s, b_bias, ev_w_out, od_w_in, od_b_f, c_conv_w, d_q_norm, d_k_norm, od_w_out, norm_ffn2, ffn2_wi, ffn2_wo, norm_ple, ple_w_gate, ple_w_proj):
    bp, s_len, d = x_prompt.shape
    bs, t_len, _ = x_sample.shape
    depth = norm_ffn1.shape[0]
    n_pool, _, page, h_a, hd = cache_a_k.shape
    h_d = cache_d_k.shape[3]
    n_pages = page_table.shape[1]
    d_a, d_d = h_a * hd, h_d * hd
    d_c = c_conv_w.shape[-1]
    assert hd == HEAD_DIM and t_len == 1 and s_len % MOBA_BLOCK == 0 and d_a == d_d == d_c
    assert (n_pages * page) % MOBA_BLOCK == 0 and b_w_s.shape[-1] == GMLP_CHUNK
    tp, ts = bp * s_len, bs * t_len
    nb = s_len // MOBA_BLOCK

    bf = lambda w: w.astype(BF16)
    vec = lambda g: g.reshape(g.shape[0], 1, -1)
    ffn1_wi, ffn1_wo, ffn2_wi, ffn2_wo = bf(ffn1_wi), bf(ffn1_wo), bf(ffn2_wi), bf(ffn2_wo)
    ple_wg, ple_wp = bf(ple_w_gate), bf(ple_w_proj)
    ev_w, ev_wo, od_wo = bf(ev_w_in), bf(ev_w_out), bf(od_w_out)
    n_main = 3 * d_c + 3 * d_d
    od_w = bf(od_w_in[:, :, :n_main])
    od_wf = od_w_in[:, :, n_main:]
    od_wf = bf(jnp.pad(od_wf, ((0, 0), (0, 0), (0, LANES - h_d))))
    od_bf = jnp.pad(od_b_f, ((0, 0), (0, LANES - h_d)))[:, None, :]
    g_ffn1, g_mix, g_ffn2, g_ple = vec(norm_ffn1), vec(norm_mix), vec(norm_ffn2), vec(norm_ple)
    a_qg = jnp.tile(a_q_norm, (1, h_a))[:, None, :]
    a_kg = jnp.tile(a_k_norm, (1, h_a))[:, None, :]
    b_vn = vec(b_v_norm)
    d_qg = jnp.tile(d_q_norm, (1, h_d))[:, None, :]
    d_kg = jnp.tile(d_k_norm, (1, h_d))[:, None, :]
    conv_w = jnp.pad(c_conv_w, ((0, 0), (0, SUBLANES - CONV_W), (0, 0)))
    gate_bias = jnp.repeat(jnp.swapaxes(b_bias, 1, 2), HEAD_DIM, axis=2)
    gate_w1 = jnp.repeat(b_w_s[:, :, 0, 0], HEAD_DIM, axis=1)[:, None, :]
    gate_b1 = jnp.repeat(b_bias[:, :, 0], HEAD_DIM, axis=1)[:, None, :]
    e_ones = _head_ones(d_a)
    place = _decay_placement(h_d)

    pos_minor = lambda c: jnp.transpose(c, (0, 1, 3, 4, 2)).reshape(n_pool, c.shape[1], -1, page)
    ck_a, cv_a, ck_d, cv_d = pos_minor(cache_a_k), pos_minor(cache_a_v), pos_minor(cache_d_k), pos_minor(cache_d_v)
    cf_d = jnp.swapaxes(cache_d_logf, 2, 3)
    pp = p_prompt.reshape(depth, tp, -1)
    ps = p_sample.reshape(depth, ts, -1)

    xp = x_prompt.reshape(tp, d)
    xs = x_sample.reshape(ts, d)
    ak_p, av_p, ak_s, av_s, bv_s = [], [], [], [], []
    cv_p, cv_s = [], []
    dk_p, dv_p, df_p, dk_s, dv_s, df_s = [], [], [], [], [], []

    for l in range(depth):
        xp = _ffn_half(xp, g_ffn1, ffn1_wi, ffn1_wo, l)
        xs = _ffn_half(xs, g_ffn1, ffn1_wi, ffn1_wo, l)
        if l % 2 == 0:
            e = l // 2
            q, k, v, u, vg, kaug, vb, km = _even_in(xp, g_mix, ev_w, a_qg, a_kg, b_vn, e_ones, l, e, True, s_len)
            km = jnp.pad(km.reshape(bp, nb, d_a), ((0, 0), (0, LANES - nb), (0, 0))).reshape(bp * LANES, d_a)
            att = _flash(q, kaug, vb, km, bp, s_len)
            sg = _gmlp_gate(u, vg, b_w_s, gate_bias, e)
            xp = _mix_out(xp, (att, sg), ev_wo, e)
            ak_p.append(k.reshape(bp, s_len, h_a, hd))
            av_p.append(v.reshape(bp, s_len, h_a, hd))

            q, k, v, u, vg = _even_in(xs, g_mix, ev_w, a_qg, a_kg, b_vn, e_ones, l, e, False, s_len)
            r3 = lambda a: a.reshape(bs, 1, -1)
            att = _paged_attn(page_table, r3(q), r3(k), r3(v), ck_a, cv_a, e).reshape(ts, d_a)
            xs = _mix_out(xs, (att, u, vg), ev_wo, e, gate_vecs=(gate_w1[e], gate_b1[e]))
            ak_s.append(k.reshape(bs, t_len, h_a, hd))
            av_s.append(v.reshape(bs, t_len, h_a, hd))
            bv_s.append(vg.reshape(bs, t_len, -1))
        else:
            o = l // 2
            yc, q, k, v, lf, kaug, vb, tail = _odd_in_prompt(
                xp, g_mix, od_w, od_wf, od_bf, d_qg, d_kg, conv_w, e_ones, place, l, o, s_len, h_d)
            att = _flash(q, kaug, vb, None, bp, s_len)
            xp = _mix_out(xp, (yc, att), od_wo, o)
            cv_p.append(tail[:, SUBLANES - (CONV_W - 1):, :])
            dk_p.append(k.reshape(bp, s_len, h_d, hd))
            dv_p.append(v.reshape(bp, s_len, h_d, hd))
            df_p.append(lf.reshape(bp, s_len, h_d))

            buf = state_c_conv[:, o]
            yc, q, k, v, lf, pre = _odd_in_sample(
                xs, g_mix, od_w, od_wf, od_bf, d_qg, d_kg, conv_w, e_ones,
                buf[:, 0], buf[:, 1], l, o)
            r3 = lambda a: a.reshape(bs, 1, -1)
            lf = lf[:, :h_d]
            att = _paged_attn(page_table, r3(q), r3(k), r3(v), ck_d, cv_d, o,
                              lf_new=lf.reshape(bs, h_d, 1), cache_f=cf_d).reshape(ts, d_d)
            xs = _mix_out(xs, (yc, att), od_wo, o)
            cv_s.append(jnp.stack([buf[:, 1], pre], axis=1))
            dk_s.append(k.reshape(bs, t_len, h_d, hd))
            dv_s.append(v.reshape(bs, t_len, h_d, hd))
            df_s.append(lf.reshape(bs, t_len, h_d))
        xp = _ffn_half(xp, g_ffn2, ffn2_wi, ffn2_wo, l)
        xs = _ffn_half(xs, g_ffn2, ffn2_wi, ffn2_wo, l)
        xp = _ple_add(xp, pp, g_ple, ple_wg, ple_wp, l)
        xs = _ple_add(xs, ps, g_ple, ple_wg, ple_wp, l)

    st = lambda rows: jnp.stack(rows, axis=1)
    return (xp.reshape(bp, s_len, d), xs.reshape(bs, t_len, d),
            st(ak_p), st(av_p), st(ak_s), st(av_s), st(bv_s),
            st(cv_p), st(cv_s),
            st(dk_p), st(dv_p), st(df_p), st(dk_s), st(dv_s), st(df_s))
```

```python
import functools
import math

import numpy as np
import jax
import jax.numpy as jnp
from jax import lax
from jax.experimental import pallas as pl
from jax.experimental.pallas import tpu as pltpu

F32 = jnp.float32
BF16 = jnp.bfloat16

HEAD_DIM = 64
MOBA_BLOCK = 256
MOBA_TOPK = 3
GMLP_CHUNK = 128
CONV_W = 3
RMS_EPS = 1e-6
NEG_INF = -1e30
ATTN_SCALE = 1.0 / math.sqrt(HEAD_DIM)
LOG2E = math.log2(math.e)

LANES = 128
SUBLANES = 8
HEADS_PER_TILE = LANES // HEAD_DIM
DECAY_LANE0 = 96
DECAY_TERMS = 3
VMEM_LIMIT = 56 * 1024 * 1024
TOKEN_TILES = (512, 256, 128, 64, 32, 16, 8)
HIGHEST = lax.Precision.HIGHEST


def _params(*sem):
    return pltpu.CompilerParams(dimension_semantics=sem, vmem_limit_bytes=VMEM_LIMIT)


def _pick(n, options):
    for t in options:
        if n % t == 0:
            return t
    raise ValueError(f"no tile in {options} divides {n}")


def _rms(x, g):
    return x * lax.rsqrt(jnp.mean(x * x, axis=-1, keepdims=True) + RMS_EPS) * g


def _dot(a, b):
    return jnp.dot(a, b, preferred_element_type=F32)


def _dot_t(a, b, precision=None):
    return lax.dot_general(a, b, (((1,), (1,)), ((), ())), precision=precision,
                           preferred_element_type=F32)


def _split3(x):
    hi = x.astype(BF16)
    r = x - hi.astype(F32)
    mid = r.astype(BF16)
    lo = (r - mid.astype(F32)).astype(BF16)
    return hi, mid, lo


def _log_sigmoid(x):
    return jnp.minimum(x, 0.0) - jnp.log1p(jnp.exp(-jnp.abs(x)))


def _head_norm(t, e, gain):
    ss = _dot((t * t).astype(BF16), e)
    return t * lax.rsqrt(ss * (1.0 / HEAD_DIM) + RMS_EPS) * gain


def _ffn_body(x_ref, g_ref, wa_ref, wb_ref, wo_ref, o_ref, hn_ref, acc_ref):
    j = pl.program_id(1)

    @pl.when(j == 0)
    def _():
        hn_ref[...] = _rms(x_ref[...], g_ref[...]).astype(BF16)
        acc_ref[...] = jnp.zeros_like(acc_ref)

    hn = hn_ref[...]
    a = _dot(hn, wa_ref[...])
    b = _dot(hn, wb_ref[...])
    act = (jax.nn.silu(a) * b).astype(BF16)
    acc_ref[...] += _dot(act, wo_ref[...])

    @pl.when(j == pl.num_programs(1) - 1)
    def _():
        o_ref[...] = x_ref[...] + 0.5 * acc_ref[...]


def _ffn_half(x, g, wi, wo, layer):
    t, d = x.shape
    f = wo.shape[1]
    tm = _pick(t, TOKEN_TILES)
    tf = _pick(f, (1408, 512, 256, 128))
    nf = f // tf
    return pl.pallas_call(
        _ffn_body,
        grid=(t // tm, nf),
        in_specs=[
            pl.BlockSpec((tm, d), lambda i, j: (i, 0)),
            pl.BlockSpec((None, 1, d), lambda i, j: (layer, 0, 0)),
            pl.BlockSpec((None, d, tf), lambda i, j: (layer, 0, j)),
            pl.BlockSpec((None, d, tf), lambda i, j: (layer, 0, j + nf)),
            pl.BlockSpec((None, tf, d), lambda i, j: (layer, j, 0)),
        ],
        out_specs=pl.BlockSpec((tm, d), lambda i, j: (i, 0)),
        out_shape=jax.ShapeDtypeStruct((t, d), F32),
        scratch_shapes=[pltpu.VMEM((tm, d), BF16), pltpu.VMEM((tm, d), F32)],
        compiler_params=_params("parallel", "arbitrary"),
        name="ffn_half",
    )(x, g, wi, wi, wo)


def _ple_body(x_ref, p_ref, g_ref, wg_ref, wp_ref, o_ref):
    x = x_ref[...]
    h = _rms(x, g_ref[...]).astype(BF16)
    gate = jax.nn.sigmoid(_dot(h, wg_ref[...]))
    proj = _dot(p_ref[...].astype(BF16), wp_ref[...])
    o_ref[...] = x + gate * proj


def _ple_add(x, p, g, wg, wp, layer):
    t, d = x.shape
    pdim = p.shape[-1]
    tm = _pick(t, TOKEN_TILES)
    return pl.pallas_call(
        _ple_body,
        grid=(t // tm,),
        in_specs=[
            pl.BlockSpec((tm, d), lambda i: (i, 0)),
            pl.BlockSpec((None, tm, pdim), lambda i: (layer, i, 0)),
            pl.BlockSpec((None, 1, d), lambda i: (layer, 0, 0)),
            pl.BlockSpec((None, d, d), lambda i: (layer, 0, 0)),
            pl.BlockSpec((None, pdim, d), lambda i: (layer, 0, 0)),
        ],
        out_specs=pl.BlockSpec((tm, d), lambda i: (i, 0)),
        out_shape=jax.ShapeDtypeStruct((t, d), F32),
        compiler_params=_params("parallel"),
        name="ple_add",
    )(x, p, g, wg, wp)


def _mix_out_body(x_ref, a_ref, b_ref, w_ref, o_ref):
    half = a_ref.shape[-1]
    o_ref[...] = (x_ref[...]
                  + _dot(a_ref[...].astype(BF16), w_ref[0:half, :])
                  + _dot(b_ref[...].astype(BF16), w_ref[half:2 * half, :]))


def _mix_out_gate_body(x_ref, a_ref, u_ref, vg_ref, wv_ref, bv_ref, w_ref, o_ref):
    half = a_ref.shape[-1]
    sg = u_ref[...] * (wv_ref[...] * vg_ref[...] + bv_ref[...])
    o_ref[...] = (x_ref[...]
                  + _dot(a_ref[...].astype(BF16), w_ref[0:half, :])
                  + _dot(sg.astype(BF16), w_ref[half:2 * half, :]))


def _mix_out(x, parts, w, layer, gate_vecs=None):
    t, d = x.shape
    half = parts[0].shape[-1]
    tm = _pick(t, TOKEN_TILES)
    row = lambda width: pl.BlockSpec((tm, width), lambda i: (i, 0))
    vec = pl.BlockSpec((1, half), lambda i: (0, 0))
    w_spec = pl.BlockSpec((None, 2 * half, d), lambda i: (layer, 0, 0))
    if gate_vecs is None:
        body, in_specs, args = _mix_out_body, [row(d), row(half), row(half), w_spec], (x, *parts, w)
    else:
        body = _mix_out_gate_body
        in_specs = [row(d), row(half), row(half), row(half), vec, vec, w_spec]
        args = (x, *parts, *gate_vecs, w)
    return pl.pallas_call(
        body,
        grid=(t // tm,),
        in_specs=in_specs,
        out_specs=row(d),
        out_shape=jax.ShapeDtypeStruct((t, d), F32),
        compiler_params=_params("parallel"),
        name="mix_out",
    )(*args)


def _gmlp_body(u_ref, vg_ref, ws_ref, bias_ref, o_ref):
    tm = u_ref.shape[0]
    c = GMLP_CHUNK
    r_i = lax.broadcasted_iota(jnp.int32, (c, c), 0)
    c_i = lax.broadcasted_iota(jnp.int32, (c, c), 1)
    tril = c_i <= r_i
    first_head = c_i < HEAD_DIM
    n_groups = ws_ref.shape[0]
    w = [jnp.where(tril, ws_ref[g], 0.0).astype(BF16) for g in range(n_groups)]
    bias = bias_ref[...]
    for ci in range(tm // c):
        rows = pl.ds(ci * c, c)
        vg = vg_ref[rows, :].astype(BF16)
        tiles = []
        for t in range(n_groups // HEADS_PER_TILE):
            vt = vg[:, t * LANES:(t + 1) * LANES]
            tiles.append(jnp.where(first_head, _dot(w[2 * t], vt), _dot(w[2 * t + 1], vt)))
        mixed = jnp.concatenate(tiles, axis=1) + bias
        o_ref[rows, :] = u_ref[rows, :] * mixed


def _gmlp_gate(u, vg, ws, bias, layer):
    t, width = u.shape
    tm = _pick(t, TOKEN_TILES)
    g, c = ws.shape[1], ws.shape[2]
    row = pl.BlockSpec((tm, width), lambda i: (i, 0))
    return pl.pallas_call(
        _gmlp_body,
        grid=(t // tm,),
        in_specs=[row, row,
                  pl.BlockSpec((None, g, c, c), lambda i: (layer, 0, 0, 0)),
                  pl.BlockSpec((None, c, width), lambda i: (layer, 0, 0))],
        out_specs=row,
        out_shape=jax.ShapeDtypeStruct((t, width), F32),
        compiler_params=_params("parallel"),
        name="gmlp_gate",
    )(u, vg, ws, bias)


def _block_onehot(tile_rows, first_pos, nb):
    r = lax.broadcasted_iota(jnp.int32, (tile_rows, LANES), 0)
    c = lax.broadcasted_iota(jnp.int32, (tile_rows, LANES), 1)
    blk = lax.div(first_pos + r, MOBA_BLOCK)
    return jnp.where((c == blk) & (c < nb), 1.0, 0.0)


def _store_value_tiles(vaug_ref, v):
    rows = v.shape[0]
    lane = lax.broadcasted_iota(jnp.int32, (rows, LANES), 1)
    low = lane < HEAD_DIM
    for t in range(v.shape[1] // LANES):
        vt = v[:, t * LANES:(t + 1) * LANES]
        vaug_ref[:, 2 * t * LANES:(2 * t + 1) * LANES] = jnp.where(low, vt, 1.0).astype(BF16)
        vaug_ref[:, (2 * t + 1) * LANES:(2 * t + 2) * LANES] = jnp.where(low, 1.0, vt).astype(BF16)


def _even_in_body(x_ref, g_ref, w_ref, qg_ref, kg_ref, vn_ref, e_ref, *rest, prompt, seq_len, n_alias):
    outs = rest[n_alias:]
    q_ref, k_ref, v_ref, u_ref, vg_ref = outs[:5]
    half = q_ref.shape[-1]
    h = _rms(x_ref[...], g_ref[...]).astype(BF16)
    z = _dot(h, w_ref[...])
    e = e_ref[...]
    q = _head_norm(z[:, 0:half], e, qg_ref[...])
    k = _head_norm(z[:, half:2 * half], e, kg_ref[...])
    v = z[:, 2 * half:3 * half]
    q_ref[...] = q
    if prompt:
        k_ref[...] = k.T
        v_ref[...] = v.T
    else:
        k_ref[...] = k
        v_ref[...] = v
    u_ref[...] = jax.nn.gelu(z[:, 3 * half:4 * half])
    vg_ref[...] = _head_norm(jax.nn.gelu(z[:, 4 * half:5 * half]), e, vn_ref[...])
    if prompt:
        kaug_ref, vaug_ref, km_ref = outs[5:]
        tm = x_ref.shape[0]
        first_pos = lax.rem(pl.program_id(0) * tm, seq_len)
        onehot = _block_onehot(tm, first_pos, seq_len // MOBA_BLOCK).astype(BF16)
        kb = k.astype(BF16)
        for t in range(half // LANES):
            kaug_ref[:, 2 * t * LANES:(2 * t + 1) * LANES] = kb[:, t * LANES:(t + 1) * LANES]
            kaug_ref[:, (2 * t + 1) * LANES:(2 * t + 2) * LANES] = onehot
        _store_value_tiles(vaug_ref, v)
        for b in range(tm // MOBA_BLOCK):
            km_ref[b:b + 1, :] = jnp.mean(k[b * MOBA_BLOCK:(b + 1) * MOBA_BLOCK, :], axis=0, keepdims=True)


def _stacked_kv(stacked, n_slots, batch, width, seq_len, slot, tm, first_alias_input):
    tiles_per_seq = seq_len // tm
    spec = pl.BlockSpec((None, None, width, tm), lambda i: (i // tiles_per_seq, slot, 0, i % tiles_per_seq))
    shape = jax.ShapeDtypeStruct((batch, n_slots, width, seq_len), F32)
    if stacked is None:
        return spec, shape, [], [], {}
    any_spec = pl.BlockSpec(memory_space=pl.ANY)
    return spec, shape, [any_spec, any_spec], list(stacked), {first_alias_input: 1, first_alias_input + 1: 2}


def _even_in(x, g, w, qg, kg, vn, e, layer, ev, prompt, seq_len, stacked=None, n_slots=1):
    t, d = x.shape
    n_in = w.shape[-1]
    half = n_in // 5
    tm = _pick(t, (512, 256) if prompt else TOKEN_TILES)
    row = lambda width: pl.BlockSpec((tm, width), lambda i: (i, 0))
    vec = lambda arr: pl.BlockSpec((None, 1, arr.shape[-1]), lambda i: (ev, 0, 0))
    in_specs = [row(d),
                pl.BlockSpec((None, 1, d), lambda i: (layer, 0, 0)),
                pl.BlockSpec((None, d, n_in), lambda i: (ev, 0, 0)),
                vec(qg), vec(kg), vec(vn),
                pl.BlockSpec(e.shape, lambda i: (0, 0))]
    args = [x, g, w, qg, kg, vn, e]
    out_specs = [row(half)] * 5
    out_shape = [jax.ShapeDtypeStruct((t, half), F32)] * 5
    aliases = {}
    if prompt:
        kv_spec, kv_shape, alias_specs, alias_args, aliases = _stacked_kv(
            stacked, n_slots, t // seq_len, half, seq_len, ev, tm, len(in_specs))
        in_specs += alias_specs
        args += alias_args
        nkm = tm // MOBA_BLOCK
        out_specs = [row(half), kv_spec, kv_spec, row(half), row(half),
                     row(2 * half), row(2 * half), pl.BlockSpec((None, nkm, half), lambda i: (i, 0, 0))]
        out_shape = [out_shape[0], kv_shape, kv_shape, out_shape[0], out_shape[0],
                     jax.ShapeDtypeStruct((t, 2 * half), BF16), jax.ShapeDtypeStruct((t, 2 * half), BF16),
                     jax.ShapeDtypeStruct((t // tm, nkm, half), F32)]
    return pl.pallas_call(
        functools.partial(_even_in_body, prompt=prompt, seq_len=seq_len, n_alias=len(aliases)),
        grid=(t // tm,),
        in_specs=in_specs,
        out_specs=out_specs,
        out_shape=out_shape,
        input_output_aliases=aliases,
        compiler_params=_params("parallel"),
        name="even_in",
    )(*args)


def _odd_in_prompt_body(x_ref, g_ref, w_ref, wf_ref, bf_ref, qg_ref, kg_ref, cw_ref, e_ref, p_ref,
                        *rest, seq_len, n_alias):
    q_ref, k_ref, v_ref, yc_ref, lf_ref, kaug_ref, vaug_ref, tail_ref, pre_ref, carry_ref = rest[n_alias:]
    tm = x_ref.shape[0]
    half = q_ref.shape[-1]
    n_heads = lf_ref.shape[-1]
    tiles_per_seq = seq_len // tm
    s_idx = lax.rem(pl.program_id(0), tiles_per_seq)

    h = _rms(x_ref[...], g_ref[...]).astype(BF16)
    z = _dot(h, w_ref[...])
    e = e_ref[...]
    gb = z[:, 0:half]
    pre = z[:, half:2 * half] * z[:, 2 * half:3 * half]
    q_ref[...] = _head_norm(z[:, 3 * half:4 * half], e, qg_ref[...])
    k = _head_norm(z[:, 4 * half:5 * half], e, kg_ref[...])
    v = z[:, 5 * half:6 * half]
    k_ref[...] = k.T
    v_ref[...] = v.T
    _store_value_tiles(vaug_ref, v)

    @pl.when(s_idx == 0)
    def _():
        pre_ref[0:SUBLANES, :] = jnp.zeros((SUBLANES, half), F32)
        carry_ref[...] = jnp.zeros_like(carry_ref)

    @pl.when(s_idx != 0)
    def _():
        pre_ref[0:SUBLANES, :] = pre_ref[tm:tm + SUBLANES, :]

    pre_ref[SUBLANES:SUBLANES + tm, :] = pre
    conv = (cw_ref[0:1, :] * pre_ref[SUBLANES - 2:SUBLANES - 2 + tm, :]
            + cw_ref[1:2, :] * pre_ref[SUBLANES - 1:SUBLANES - 1 + tm, :]
            + cw_ref[2:3, :] * pre)
    yc_ref[...] = gb * conv
    tail_ref[...] = pre[tm - SUBLANES:tm, :]

    lane = lax.broadcasted_iota(jnp.int32, (tm, LANES), 1)
    logf = jnp.where(lane < n_heads, _log_sigmoid(_dot(h, wf_ref[...]) + bf_ref[...]), 0.0)
    lf_ref[...] = logf[:, 0:n_heads]
    r_i = lax.broadcasted_iota(jnp.int32, (tm, tm), 0)
    c_i = lax.broadcasted_iota(jnp.int32, (tm, tm), 1)
    tril = jnp.where(c_i <= r_i, 1.0, 0.0).astype(BF16)
    hi, mid, lo = _split3(logf)
    cum = _dot(tril, hi) + _dot(tril, mid) + _dot(tril, lo) + carry_ref[...]
    carry_ref[...] = cum[tm - 1:tm, :]

    dec = _dot(jnp.concatenate(_split3(cum * (-LOG2E)), axis=1), p_ref[...])
    onehot = _block_onehot(tm, s_idx * tm, seq_len // MOBA_BLOCK)
    kb = k.astype(BF16)
    for t in range(half // LANES):
        kaug_ref[:, 2 * t * LANES:(2 * t + 1) * LANES] = kb[:, t * LANES:(t + 1) * LANES]
        kaug_ref[:, (2 * t + 1) * LANES:(2 * t + 2) * LANES] = (
            dec[:, t * LANES:(t + 1) * LANES] + onehot).astype(BF16)


def _odd_in_prompt(x, g, w, wf, bf, qg, kg, cw, e, place, layer, od, seq_len, n_heads, stacked=None, n_slots=1):
    t, d = x.shape
    half = qg.shape[-1]
    tm = _pick(seq_len, (512, 256))
    row = lambda width: pl.BlockSpec((tm, width), lambda i: (i, 0))
    vec = lambda arr: pl.BlockSpec((None,) + arr.shape[1:], lambda i: (od, 0, 0))
    full = lambda arr: pl.BlockSpec(arr.shape, lambda i: (0, 0))
    tiles_per_seq = seq_len // tm
    in_specs = [row(d),
                pl.BlockSpec((None, 1, d), lambda i: (layer, 0, 0)),
                vec(w), vec(wf), vec(bf), vec(qg), vec(kg), vec(cw), full(e), full(place)]
    kv_spec, kv_shape, alias_specs, alias_args, aliases = _stacked_kv(
        stacked, n_slots, t // seq_len, half, seq_len, od, tm, len(in_specs))
    return pl.pallas_call(
        functools.partial(_odd_in_prompt_body, seq_len=seq_len, n_alias=len(aliases)),
        grid=(t // tm,),
        in_specs=in_specs + alias_specs,
        out_specs=[row(half), kv_spec, kv_spec, row(half), row(n_heads), row(2 * half), row(2 * half),
                   pl.BlockSpec((None, SUBLANES, half), lambda i: (i // tiles_per_seq, 0, 0))],
        out_shape=[jax.ShapeDtypeStruct((t, half), F32), kv_shape, kv_shape, jax.ShapeDtypeStruct((t, half), F32),
                   jax.ShapeDtypeStruct((t, n_heads), F32),
                   jax.ShapeDtypeStruct((t, 2 * half), BF16), jax.ShapeDtypeStruct((t, 2 * half), BF16),
                   jax.ShapeDtypeStruct((t // seq_len, SUBLANES, half), F32)],
        scratch_shapes=[pltpu.VMEM((tm + 2 * SUBLANES, half), F32), pltpu.VMEM((1, LANES), F32)],
        input_output_aliases=aliases,
        compiler_params=_params("arbitrary"),
        name="odd_in_prompt",
    )(x, g, w, wf, bf, qg, kg, cw, e, place, *alias_args)


def _odd_in_sample_body(x_ref, g_ref, w_ref, wf_ref, bf_ref, qg_ref, kg_ref, cw_ref, e_ref,
                        b0_ref, b1_ref, yc_ref, q_ref, k_ref, v_ref, lf_ref, pre_ref):
    half = q_ref.shape[-1]
    h = _rms(x_ref[...], g_ref[...]).astype(BF16)
    z = _dot(h, w_ref[...])
    e = e_ref[...]
    pre = z[:, half:2 * half] * z[:, 2 * half:3 * half]
    conv = cw_ref[0:1, :] * b0_ref[...] + cw_ref[1:2, :] * b1_ref[...] + cw_ref[2:3, :] * pre
    yc_ref[...] = z[:, 0:half] * conv
    pre_ref[...] = pre
    q_ref[...] = _head_norm(z[:, 3 * half:4 * half], e, qg_ref[...])
    k_ref[...] = _head_norm(z[:, 4 * half:5 * half], e, kg_ref[...])
    v_ref[...] = z[:, 5 * half:6 * half]
    lf_ref[...] = _log_sigmoid(_dot(h, wf_ref[...]) + bf_ref[...])


def _odd_in_sample(x, g, w, wf, bf, qg, kg, cw, e, b0, b1, layer, od):
    t, d = x.shape
    half = qg.shape[-1]
    tm = _pick(t, TOKEN_TILES)
    row = lambda width: pl.BlockSpec((tm, width), lambda i: (i, 0))
    vec = lambda arr: pl.BlockSpec((None,) + arr.shape[1:], lambda i: (od, 0, 0))
    return pl.pallas_call(
        _odd_in_sample_body,
        grid=(t // tm,),
        in_specs=[row(d),
                  pl.BlockSpec((None, 1, d), lambda i: (layer, 0, 0)),
                  vec(w), vec(wf), vec(bf), vec(qg), vec(kg), vec(cw),
                  pl.BlockSpec(e.shape, lambda i: (0, 0)), row(half), row(half)],
        out_specs=[row(half)] * 4 + [row(LANES), row(half)],
        out_shape=[jax.ShapeDtypeStruct((t, half), F32)] * 4
        + [jax.ShapeDtypeStruct((t, LANES), F32), jax.ShapeDtypeStruct((t, half), F32)],
        compiler_params=_params("parallel"),
        name="odd_in_sample",
    )(x, g, w, wf, bf, qg, kg, cw, e, b0, b1)


def _flash_body(q_ref, kaug_ref, v_ref, *rest, moba, nb, tk, n_tiles):
    if moba:
        km_ref, o_ref, qa_ref = rest
    else:
        o_ref, qa_ref = rest
    tq = MOBA_BLOCK
    qi = pl.program_id(2)
    lane = lax.broadcasted_iota(jnp.int32, (tq, LANES), 1)
    is_blk = lane < nb
    past = lane < qi
    low = lane < HEAD_DIM
    nb_rows = -(-nb // SUBLANES) * SUBLANES
    blk_t = lax.broadcasted_iota(jnp.int32, (nb_rows, tq), 0)
    past_t = blk_t < qi
    blk_f = blk_t.astype(F32)
    own_rows = pl.ds(pl.multiple_of(qi * tq, tq), tq)
    r_i = lax.broadcasted_iota(jnp.int32, (tq, tq), 0)
    c_i = lax.broadcasted_iota(jnp.int32, (tq, tq), 1)
    causal = c_i <= r_i
    chains = [(t, hh) for t in range(n_tiles) for hh in range(HEADS_PER_TILE)]

    init = []
    for c, (t, hh) in enumerate(chains):
        q32 = q_ref[:, t * LANES:(t + 1) * LANES]
        qh = jnp.where(low if hh == 0 else ~low, q32, 0.0)
        if moba:
            km = km_ref[0:nb_rows, t * LANES:(t + 1) * LANES]
            gate = jnp.where(past_t, _dot_t(km, qh, precision=HIGHEST), NEG_INF)
            bias_t = jnp.full((nb_rows, tq), NEG_INF, F32)
            for _ in range(MOBA_TOPK):
                top = jnp.max(gate, axis=0, keepdims=True)
                idx = jnp.min(jnp.where(gate == top, blk_f, float(LANES)), axis=0, keepdims=True)
                pick = blk_f == idx
                bias_t = jnp.where(pick, jnp.where(past_t, 0.0, NEG_INF), bias_t)
                gate = jnp.where(pick, -jnp.inf, gate)
            bias = jnp.concatenate([bias_t, jnp.full((LANES - nb_rows, tq), NEG_INF, F32)], axis=0).T
            aux_own = jnp.zeros((tq, LANES), F32)
        else:
            bias = jnp.where(past, 0.0, NEG_INF)
            d0 = DECAY_LANE0 + DECAY_TERMS * hh
            aux_own = jnp.where((lane >= d0) & (lane < d0 + DECAY_TERMS), 1.0, 0.0)
        aux = jnp.where(is_blk, bias, aux_own)
        qb = (qh * (ATTN_SCALE * LOG2E)).astype(BF16)
        qa_ref[c] = jnp.concatenate([qb, aux.astype(BF16)], axis=1)
        k_own = kaug_ref[own_rows, 2 * t * LANES:(2 * t + 2) * LANES]
        v_own = v_ref[own_rows, c * LANES:(c + 1) * LANES]
        s = _dot_t(jnp.concatenate([qb, aux_own.astype(BF16)], axis=1), k_own)
        s = jnp.where(causal, s, NEG_INF)
        m = jnp.max(s, axis=-1, keepdims=True)
        init.append((m, _dot(jnp.exp2(s - m).astype(BF16), v_own)))

    def body(g, carry):
        rows = pl.ds(pl.multiple_of(g * tk, tk), tk)
        new = []
        for c, (t, hh) in enumerate(chains):
            m, acc = carry[c]
            s = _dot_t(qa_ref[c], kaug_ref[rows, 2 * t * LANES:(2 * t + 2) * LANES])
            m_new = jnp.maximum(m, jnp.max(s, axis=-1, keepdims=True))
            p = jnp.exp2(s - m_new).astype(BF16)
            new.append((m_new, jnp.exp2(m - m_new) * acc + _dot(p, v_ref[rows, c * LANES:(c + 1) * LANES])))
        return tuple(new)

    blocks_per_step = tk // tq
    n_steps = lax.div(qi + (blocks_per_step - 1), blocks_per_step)
    final = lax.fori_loop(0, n_steps, body, tuple(init))
    for t in range(n_tiles):
        (_, a0), (_, a1) = final[2 * t], final[2 * t + 1]
        o_ref[:, t * LANES:(t + 1) * LANES] = jnp.where(
            low, a0 * (1.0 / pltpu.roll(a0, HEAD_DIM, 1)), a1 * (1.0 / pltpu.roll(a1, HEAD_DIM, 1)))


def _flash(q, kaug, vaug, kmean, batch, seq_len):
    t, width = q.shape
    tq = MOBA_BLOCK
    nq = seq_len // tq
    nb = seq_len // MOBA_BLOCK
    assert nb <= DECAY_LANE0 and DECAY_LANE0 + HEADS_PER_TILE * DECAY_TERMS <= LANES
    tk = _pick(seq_len, (1024, 512, 256))
    n_tiles = 2
    cols = n_tiles * LANES
    moba = kmean is not None
    in_specs = [pl.BlockSpec((tq, cols), lambda b, hp, i: (b * nq + i, hp)),
                pl.BlockSpec((seq_len, 2 * cols), lambda b, hp, i: (b, hp)),
                pl.BlockSpec((seq_len, 2 * cols), lambda b, hp, i: (b, hp))]
    args = [q, kaug, vaug]
    if moba:
        in_specs.append(pl.BlockSpec((LANES, cols), lambda b, hp, i: (b, hp)))
        args.append(kmean)
    return pl.pallas_call(
        functools.partial(_flash_body, moba=moba, nb=nb, tk=tk, n_tiles=n_tiles),
        grid=(batch, width // cols, nq),
        in_specs=in_specs,
        out_specs=pl.BlockSpec((tq, cols), lambda b, hp, i: (b * nq + i, hp)),
        out_shape=jax.ShapeDtypeStruct((t, width), F32),
        scratch_shapes=[pltpu.VMEM((n_tiles * HEADS_PER_TILE, tq, 2 * LANES), BF16)],
        compiler_params=_params("parallel", "parallel", "arbitrary"),
        name="moba_attn" if moba else "fox_attn",
    )(*args)


def _paged_body(pt_ref, q_ref, kn_ref, vn_ref, *rest, moba, n_pages, n_heads):
    del pt_ref
    if moba:
        k_refs, v_refs, (o_ref,) = rest[:n_pages], rest[n_pages:2 * n_pages], rest[2 * n_pages:]
    else:
        lfn_ref = rest[0]
        k_refs, v_refs = rest[1:1 + n_pages], rest[1 + n_pages:1 + 2 * n_pages]
        f_refs, (o_ref,) = rest[1 + 2 * n_pages:1 + 3 * n_pages], rest[1 + 3 * n_pages:]
    page = k_refs[0].shape[1]
    width = q_ref.shape[-1]
    assert n_heads == SUBLANES
    q = q_ref[...]
    h_i = lax.broadcasted_iota(jnp.int32, (n_heads, width), 0)
    w_i = lax.broadcasted_iota(jnp.int32, (n_heads, width), 1)
    own = lax.div(w_i, HEAD_DIM) == h_i
    q8 = jnp.where(own, q, 0.0)
    h2_i = lax.broadcasted_iota(jnp.int32, (2 * n_heads, width), 0)
    w2_i = lax.broadcasted_iota(jnp.int32, (2 * n_heads, width), 1)
    qs = jnp.where(lax.div(w2_i, HEAD_DIM) == lax.rem(h2_i, n_heads), q, 0.0) * ATTN_SCALE
    qs_hi = qs.astype(BF16).astype(F32)
    q16 = jnp.where(h2_i < n_heads, qs_hi, qs - qs_hi).astype(BF16)
    s = []
    for p in range(n_pages):
        s16 = _dot(q16, k_refs[p][...].astype(BF16))
        s.append(s16[0:n_heads, :] + s16[n_heads:2 * n_heads, :])

    if moba:
        pages_per_blk = MOBA_BLOCK // page
        n_blk = n_pages // pages_per_blk
        assert MOBA_TOPK <= n_blk <= LANES
        lane = lax.broadcasted_iota(jnp.int32, (n_heads, LANES), 1)
        lane_f = lane.astype(F32)
        gate = jnp.full((n_heads, LANES), NEG_INF, F32)
        for b in range(n_blk):
            tot = sum(jnp.sum(s[b * pages_per_blk + j], axis=-1, keepdims=True) for j in range(pages_per_blk))
            gate = jnp.where(lane == b, tot, gate)
        chosen = jnp.zeros((n_heads, LANES), F32)
        for _ in range(MOBA_TOPK):
            top = jnp.max(gate, axis=-1, keepdims=True)
            idx = jnp.min(jnp.where(gate == top, lane_f, float(LANES)), axis=-1, keepdims=True)
            pick = lane_f == idx
            chosen = jnp.where(pick, 1.0, chosen)
            gate = jnp.where(pick, -jnp.inf, gate)
        for p in range(n_pages):
            on = jnp.max(jnp.where(lane == p // pages_per_blk, chosen, 0.0), axis=-1, keepdims=True)
            s[p] = jnp.where(on > 0.0, s[p], NEG_INF)
    else:
        r_i = lax.broadcasted_iota(jnp.int32, (page, page), 0)
        c_i = lax.broadcasted_iota(jnp.int32, (page, page), 1)
        after_in_page = jnp.where(r_i > c_i, 1.0, 0.0).astype(BF16)
        after = lfn_ref[...]
        for p in reversed(range(n_pages)):
            f = f_refs[p][...]
            hi, mid, lo = _split3(f)
            within = _dot(hi, after_in_page) + _dot(mid, after_in_page) + _dot(lo, after_in_page)
            s[p] = s[p] + (within + after)
            after = after + jnp.sum(f, axis=-1, keepdims=True)

    s_self = jnp.sum(q8 * kn_ref[...], axis=-1, keepdims=True) * ATTN_SCALE
    m = s_self
    for p in range(n_pages):
        m = jnp.maximum(m, jnp.max(s[p], axis=-1, keepdims=True))
    p_self = jnp.exp(s_self - m)
    l = p_self
    acc = p_self * vn_ref[...]
    for p in range(n_pages):
        e = jnp.exp(s[p] - m)
        l = l + jnp.sum(e, axis=-1, keepdims=True)
        acc = acc + _dot_t(e.astype(BF16), v_refs[p][...].astype(BF16))
    o_ref[...] = jnp.sum(jnp.where(own, acc * (1.0 / l), 0.0), axis=0, keepdims=True)


def _paged_attn(page_table, q, kn, vn, cache_k, cache_v, layer_idx, lf_new=None, cache_f=None):
    n, _, width = q.shape
    n_pages = page_table.shape[1]
    n_heads = width // HEAD_DIM
    moba = cache_f is None
    one = pl.BlockSpec((None, 1, width), lambda s, pt: (s, 0, 0))

    def paged(arr, p):
        return pl.BlockSpec((None, None) + arr.shape[2:], lambda s, pt: (pt[s, p], layer_idx, 0, 0))

    in_specs = [one, one, one]
    args = [q, kn, vn]
    if not moba:
        in_specs.append(pl.BlockSpec((None, n_heads, 1), lambda s, pt: (s, 0, 0)))
        args.append(lf_new)
    in_specs += [paged(cache_k, p) for p in range(n_pages)] + [paged(cache_v, p) for p in range(n_pages)]
    args += [cache_k] * n_pages + [cache_v] * n_pages
    if not moba:
        in_specs += [paged(cache_f, p) for p in range(n_pages)]
        args += [cache_f] * n_pages
    return pl.pallas_call(
        functools.partial(_paged_body, moba=moba, n_pages=n_pages, n_heads=n_heads),
        grid_spec=pltpu.PrefetchScalarGridSpec(
            num_scalar_prefetch=1,
            grid=(n,),
            in_specs=in_specs,
            out_specs=pl.BlockSpec((None, 1, width), lambda s, pt: (s, 0, 0)),
        ),
        out_shape=jax.ShapeDtypeStruct((n, 1, width), F32),
        compiler_params=_params("arbitrary"),
        name="moba_paged" if moba else "fox_paged",
    )(page_table, *args)


def _head_ones(width):
    i = np.arange(width) // HEAD_DIM
    return jnp.asarray(i[:, None] == i[None, :], dtype=BF16)


def _decay_placement(n_heads):
    place = np.zeros((DECAY_TERMS * LANES, n_heads // HEADS_PER_TILE * LANES), np.float32)
    for c in range(DECAY_TERMS):
        for h in range(n_heads):
            col = (h // HEADS_PER_TILE) * LANES + DECAY_LANE0 + DECAY_TERMS * (h % HEADS_PER_TILE) + c
            place[c * LANES + h, col] = 1.0
    return jnp.asarray(place, dtype=BF16)


def kernel(x_prompt, x_sample, cache_a_k, cache_a_v, cache_d_k, cache_d_v, cache_d_logf, state_c_conv, page_table, p_prompt, p_sample, norm_ffn1, ffn1_wi, ffn1_wo, norm_mix, ev_w_in, a_q_norm, a_k_norm, b_v_norm, b_w_s, b_bias, ev_w_out, od_w_in, od_b_f, c_conv_w, d_q_norm, d_k_norm, od_w_out, norm_ffn2, ffn2_wi, ffn2_wo, norm_ple, ple_w_gate, ple_w_proj):
    bp, s_len, d = x_prompt.shape
    bs, t_len, _ = x_sample.shape
    depth = norm_ffn1.shape[0]
    n_pool, _, page, h_a, hd = cache_a_k.shape
    h_d = cache_d_k.shape[3]
    n_pages = page_table.shape[1]
    d_a, d_d = h_a * hd, h_d * hd
    d_c = c_conv_w.shape[-1]
    assert hd == HEAD_DIM and t_len == 1 and s_len % MOBA_BLOCK == 0 and d_a == d_d == d_c
    assert (n_pages * page) % MOBA_BLOCK == 0 and b_w_s.shape[-1] == GMLP_CHUNK
    tp, ts = bp * s_len, bs * t_len
    nb = s_len // MOBA_BLOCK

    bf = lambda w: w.astype(BF16)
    vec = lambda g: g.reshape(g.shape[0], 1, -1)
    ffn1_wi, ffn1_wo, ffn2_wi, ffn2_wo = bf(ffn1_wi), bf(ffn1_wo), bf(ffn2_wi), bf(ffn2_wo)
    ple_wg, ple_wp = bf(ple_w_gate), bf(ple_w_proj)
    ev_w, ev_wo, od_wo = bf(ev_w_in), bf(ev_w_out), bf(od_w_out)
    n_main = 3 * d_c + 3 * d_d
    od_w = bf(od_w_in[:, :, :n_main])
    od_wf = od_w_in[:, :, n_main:]
    od_wf = bf(jnp.pad(od_wf, ((0, 0), (0, 0), (0, LANES - h_d))))
    od_bf = jnp.pad(od_b_f, ((0, 0), (0, LANES - h_d)))[:, None, :]
    g_ffn1, g_mix, g_ffn2, g_ple = vec(norm_ffn1), vec(norm_mix), vec(norm_ffn2), vec(norm_ple)
    a_qg = jnp.tile(a_q_norm, (1, h_a))[:, None, :]
    a_kg = jnp.tile(a_k_norm, (1, h_a))[:, None, :]
    b_vn = vec(b_v_norm)
    d_qg = jnp.tile(d_q_norm, (1, h_d))[:, None, :]
    d_kg = jnp.tile(d_k_norm, (1, h_d))[:, None, :]
    conv_w = jnp.pad(c_conv_w, ((0, 0), (0, SUBLANES - CONV_W), (0, 0)))
    gate_bias = jnp.repeat(jnp.swapaxes(b_bias, 1, 2), HEAD_DIM, axis=2)
    gate_w1 = jnp.repeat(b_w_s[:, :, 0, 0], HEAD_DIM, axis=1)[:, None, :]
    gate_b1 = jnp.repeat(b_bias[:, :, 0], HEAD_DIM, axis=1)[:, None, :]
    e_ones = _head_ones(d_a)
    place = _decay_placement(h_d)

    pos_minor = lambda c: jnp.transpose(c, (0, 1, 3, 4, 2)).reshape(n_pool, c.shape[1], -1, page)
    ck_a, cv_a, ck_d, cv_d = pos_minor(cache_a_k), pos_minor(cache_a_v), pos_minor(cache_d_k), pos_minor(cache_d_v)
    cf_d = jnp.swapaxes(cache_d_logf, 2, 3)
    pp = p_prompt.reshape(depth, tp, -1)
    ps = p_sample.reshape(depth, ts, -1)

    xp = x_prompt.reshape(tp, d)
    xs = x_sample.reshape(ts, d)
    n_even, n_odd = (depth + 1) // 2, depth // 2
    a_kv = d_kv = None
    ak_s, av_s, bv_s = [], [], []
    cv_p, cv_s = [], []
    df_p, dk_s, dv_s, df_s = [], [], [], []

    for l in range(depth):
        xp = _ffn_half(xp, g_ffn1, ffn1_wi, ffn1_wo, l)
        xs = _ffn_half(xs, g_ffn1, ffn1_wi, ffn1_wo, l)
        if l % 2 == 0:
            e = l // 2
            q, kt, vt, u, vg, kaug, vaug, km = _even_in(
                xp, g_mix, ev_w, a_qg, a_kg, b_vn, e_ones, l, e, True, s_len, stacked=a_kv, n_slots=n_even)
            a_kv = (kt, vt)
            km = jnp.pad(km.reshape(bp, nb, d_a), ((0, 0), (0, LANES - nb), (0, 0))).reshape(bp * LANES, d_a)
            att = _flash(q, kaug, vaug, km, bp, s_len)
            sg = _gmlp_gate(u, vg, b_w_s, gate_bias, e)
            xp = _mix_out(xp, (att, sg), ev_wo, e)

            q, k, v, u, vg = _even_in(xs, g_mix, ev_w, a_qg, a_kg, b_vn, e_ones, l, e, False, s_len)
            r3 = lambda a: a.reshape(bs, 1, -1)
            att = _paged_attn(page_table, r3(q), r3(k), r3(v), ck_a, cv_a, e).reshape(ts, d_a)
            xs = _mix_out(xs, (att, u, vg), ev_wo, e, gate_vecs=(gate_w1[e], gate_b1[e]))
            ak_s.append(k.reshape(bs, t_len, h_a, hd))
            av_s.append(v.reshape(bs, t_len, h_a, hd))
            bv_s.append(vg.reshape(bs, t_len, -1))
        else:
            o = l // 2
            q, kt, vt, yc, lf, kaug, vaug, tail = _odd_in_prompt(
                xp, g_mix, od_w, od_wf, od_bf, d_qg, d_kg, conv_w, e_ones, place, l, o, s_len, h_d,
                stacked=d_kv, n_slots=n_odd)
            d_kv = (kt, vt)
            att = _flash(q, kaug, vaug, None, bp, s_len)
            xp = _mix_out(xp, (yc, att), od_wo, o)
            cv_p.append(tail[:, SUBLANES - (CONV_W - 1):, :])
            df_p.append(lf.reshape(bp, s_len, h_d))

            buf = state_c_conv[:, o]
            yc, q, k, v, lf, pre = _odd_in_sample(
                xs, g_mix, od_w, od_wf, od_bf, d_qg, d_kg, conv_w, e_ones,
                buf[:, 0], buf[:, 1], l, o)
            r3 = lambda a: a.reshape(bs, 1, -1)
            lf = lf[:, :h_d]
            att = _paged_attn(page_table, r3(q), r3(k), r3(v), ck_d, cv_d, o,
                              lf_new=lf.reshape(bs, h_d, 1), cache_f=cf_d).reshape(ts, d_d)
            xs = _mix_out(xs, (yc, att), od_wo, o)
            cv_s.append(jnp.stack([buf[:, 1], pre], axis=1))
            dk_s.append(k.reshape(bs, t_len, h_d, hd))
            dv_s.append(v.reshape(bs, t_len, h_d, hd))
            df_s.append(lf.reshape(bs, t_len, h_d))
        xp = _ffn_half(xp, g_ffn2, ffn2_wi, ffn2_wo, l)
        xs = _ffn_half(xs, g_ffn2, ffn2_wi, ffn2_wo, l)
        xp = _ple_add(xp, pp, g_ple, ple_wg, ple_wp, l)
        xs = _ple_add(xs, ps, g_ple, ple_wg, ple_wp, l)

    st = lambda rows: jnp.stack(rows, axis=1)
    seq_major = lambda a, heads: jnp.transpose(a.reshape(bp, a.shape[1], heads, hd, s_len), (0, 1, 4, 2, 3))
    return (xp.reshape(bp, s_len, d), xs.reshape(bs, t_len, d),
            seq_major(a_kv[0], h_a), seq_major(a_kv[1], h_a), st(ak_s), st(av_s), st(bv_s),
            st(cv_p), st(cv_s),
            seq_major(d_kv[0], h_d), seq_major(d_kv[1], h_d), st(df_p), st(dk_s), st(dv_s), st(df_s))
```

```python
import functools
import math

import numpy as np
import jax
import jax.numpy as jnp
from jax import lax
from jax.experimental import pallas as pl
from jax.experimental.pallas import tpu as pltpu

F32 = jnp.float32
BF16 = jnp.bfloat16

HEAD_DIM = 64
MOBA_BLOCK = 256
MOBA_TOPK = 3
GMLP_CHUNK = 128
CONV_W = 3
RMS_EPS = 1e-6
NEG_INF = -1e30
ATTN_SCALE = 1.0 / math.sqrt(HEAD_DIM)
LOG2E = math.log2(math.e)

LANES = 128
SUBLANES = 8
HEADS_PER_TILE = LANES // HEAD_DIM
DECAY_LANE0 = 96
DECAY_TERMS = 3
PAIR_ROWS = LANES + 16
VMEM_LIMIT = 56 * 1024 * 1024
TOKEN_TILES = (512, 256, 128, 64, 32, 16, 8)
HIGHEST = lax.Precision.HIGHEST


def _params(*sem):
    return pltpu.CompilerParams(dimension_semantics=sem, vmem_limit_bytes=VMEM_LIMIT)


def _pick(n, options):
    for t in options:
        if n % t == 0:
            return t
    raise ValueError(f"no tile in {options} divides {n}")


def _rms(x, g):
    return x * lax.rsqrt(jnp.mean(x * x, axis=-1, keepdims=True) + RMS_EPS) * g


def _dot(a, b):
    return jnp.dot(a, b, preferred_element_type=F32)


def _dot_t(a, b, precision=None):
    return lax.dot_general(a, b, (((1,), (1,)), ((), ())), precision=precision,
                           preferred_element_type=F32)


def _split3(x):
    hi = x.astype(BF16)
    r = x - hi.astype(F32)
    mid = r.astype(BF16)
    lo = (r - mid.astype(F32)).astype(BF16)
    return hi, mid, lo


def _log_sigmoid(x):
    return jnp.minimum(x, 0.0) - jnp.log1p(jnp.exp(-jnp.abs(x)))


def _head_norm(t, e, gain):
    ss = _dot((t * t).astype(BF16), e)
    return t * lax.rsqrt(ss * (1.0 / HEAD_DIM) + RMS_EPS) * gain


def _ffn_body(x_ref, g_ref, wa_ref, wb_ref, wo_ref, o_ref, hn_ref, acc_ref):
    j = pl.program_id(1)

    @pl.when(j == 0)
    def _():
        hn_ref[...] = _rms(x_ref[...], g_ref[...]).astype(BF16)
        acc_ref[...] = jnp.zeros_like(acc_ref)

    hn = hn_ref[...]
    a = _dot(hn, wa_ref[...])
    b = _dot(hn, wb_ref[...])
    act = (jax.nn.silu(a) * b).astype(BF16)
    acc_ref[...] += _dot(act, wo_ref[...])

    @pl.when(j == pl.num_programs(1) - 1)
    def _():
        o_ref[...] = x_ref[...] + 0.5 * acc_ref[...]


def _ffn_half(x, g, wi, wo, layer):
    t, d = x.shape
    f = wo.shape[1]
    tm = _pick(t, TOKEN_TILES)
    tf = _pick(f, (1408, 512, 256, 128))
    nf = f // tf
    return pl.pallas_call(
        _ffn_body,
        grid=(t // tm, nf),
        in_specs=[
            pl.BlockSpec((tm, d), lambda i, j: (i, 0)),
            pl.BlockSpec((None, 1, d), lambda i, j: (layer, 0, 0)),
            pl.BlockSpec((None, d, tf), lambda i, j: (layer, 0, j)),
            pl.BlockSpec((None, d, tf), lambda i, j: (layer, 0, j + nf)),
            pl.BlockSpec((None, tf, d), lambda i, j: (layer, j, 0)),
        ],
        out_specs=pl.BlockSpec((tm, d), lambda i, j: (i, 0)),
        out_shape=jax.ShapeDtypeStruct((t, d), F32),
        scratch_shapes=[pltpu.VMEM((tm, d), BF16), pltpu.VMEM((tm, d), F32)],
        compiler_params=_params("parallel", "arbitrary"),
        name="ffn_half",
    )(x, g, wi, wi, wo)


def _ple_body(x_ref, p_ref, g_ref, wg_ref, wp_ref, o_ref):
    x = x_ref[...]
    h = _rms(x, g_ref[...]).astype(BF16)
    gate = jax.nn.sigmoid(_dot(h, wg_ref[...]))
    proj = _dot(p_ref[...].astype(BF16), wp_ref[...])
    o_ref[...] = x + gate * proj


def _ple_add(x, p, g, wg, wp, layer):
    t, d = x.shape
    pdim = p.shape[-1]
    tm = _pick(t, TOKEN_TILES)
    return pl.pallas_call(
        _ple_body,
        grid=(t // tm,),
        in_specs=[
            pl.BlockSpec((tm, d), lambda i: (i, 0)),
            pl.BlockSpec((None, tm, pdim), lambda i: (layer, i, 0)),
            pl.BlockSpec((None, 1, d), lambda i: (layer, 0, 0)),
            pl.BlockSpec((None, d, d), lambda i: (layer, 0, 0)),
            pl.BlockSpec((None, pdim, d), lambda i: (layer, 0, 0)),
        ],
        out_specs=pl.BlockSpec((tm, d), lambda i: (i, 0)),
        out_shape=jax.ShapeDtypeStruct((t, d), F32),
        compiler_params=_params("parallel"),
        name="ple_add",
    )(x, p, g, wg, wp)


def _mix_out_body(x_ref, a_ref, b_ref, w_ref, o_ref):
    half = a_ref.shape[-1]
    o_ref[...] = (x_ref[...]
                  + _dot(a_ref[...].astype(BF16), w_ref[0:half, :])
                  + _dot(b_ref[...].astype(BF16), w_ref[half:2 * half, :]))


def _mix_out_gate_body(x_ref, a_ref, u_ref, vg_ref, wv_ref, bv_ref, w_ref, o_ref):
    half = a_ref.shape[-1]
    sg = u_ref[...] * (wv_ref[...] * vg_ref[...] + bv_ref[...])
    o_ref[...] = (x_ref[...]
                  + _dot(a_ref[...].astype(BF16), w_ref[0:half, :])
                  + _dot(sg.astype(BF16), w_ref[half:2 * half, :]))


def _mix_out(x, parts, w, layer, gate_vecs=None):
    t, d = x.shape
    half = parts[0].shape[-1]
    tm = _pick(t, TOKEN_TILES)
    row = lambda width: pl.BlockSpec((tm, width), lambda i: (i, 0))
    vec = pl.BlockSpec((1, half), lambda i: (0, 0))
    w_spec = pl.BlockSpec((None, 2 * half, d), lambda i: (layer, 0, 0))
    if gate_vecs is None:
        body, in_specs, args = _mix_out_body, [row(d), row(half), row(half), w_spec], (x, *parts, w)
    else:
        body = _mix_out_gate_body
        in_specs = [row(d), row(half), row(half), row(half), vec, vec, w_spec]
        args = (x, *parts, *gate_vecs, w)
    return pl.pallas_call(
        body,
        grid=(t // tm,),
        in_specs=in_specs,
        out_specs=row(d),
        out_shape=jax.ShapeDtypeStruct((t, d), F32),
        compiler_params=_params("parallel"),
        name="mix_out",
    )(*args)


def _gmlp_body(u_ref, vg_ref, ws_ref, bias_ref, o_ref):
    tm = u_ref.shape[0]
    c = GMLP_CHUNK
    r_i = lax.broadcasted_iota(jnp.int32, (c, c), 0)
    c_i = lax.broadcasted_iota(jnp.int32, (c, c), 1)
    tril = c_i <= r_i
    first_head = c_i < HEAD_DIM
    n_groups = ws_ref.shape[0]
    w = [jnp.where(tril, ws_ref[g], 0.0).astype(BF16) for g in range(n_groups)]
    bias = bias_ref[...]
    for ci in range(tm // c):
        rows = pl.ds(ci * c, c)
        vg = vg_ref[rows, :].astype(BF16)
        tiles = []
        for t in range(n_groups // HEADS_PER_TILE):
            vt = vg[:, t * LANES:(t + 1) * LANES]
            tiles.append(jnp.where(first_head, _dot(w[2 * t], vt), _dot(w[2 * t + 1], vt)))
        mixed = jnp.concatenate(tiles, axis=1) + bias
        o_ref[rows, :] = u_ref[rows, :] * mixed


def _gmlp_gate(u, vg, ws, bias, layer):
    t, width = u.shape
    tm = _pick(t, TOKEN_TILES)
    g, c = ws.shape[1], ws.shape[2]
    row = pl.BlockSpec((tm, width), lambda i: (i, 0))
    return pl.pallas_call(
        _gmlp_body,
        grid=(t // tm,),
        in_specs=[row, row,
                  pl.BlockSpec((None, g, c, c), lambda i: (layer, 0, 0, 0)),
                  pl.BlockSpec((None, c, width), lambda i: (layer, 0, 0))],
        out_specs=row,
        out_shape=jax.ShapeDtypeStruct((t, width), F32),
        compiler_params=_params("parallel"),
        name="gmlp_gate",
    )(u, vg, ws, bias)


def _block_onehot(tile_rows, first_pos, nb):
    r = lax.broadcasted_iota(jnp.int32, (tile_rows, LANES), 0)
    c = lax.broadcasted_iota(jnp.int32, (tile_rows, LANES), 1)
    blk = lax.div(first_pos + r, MOBA_BLOCK)
    return jnp.where((c == blk) & (c < nb), 1.0, 0.0)


def _store_value_tiles(vaug_ref, v_t):
    ones = jnp.ones((PAIR_ROWS - LANES, MOBA_BLOCK), BF16)
    for j in range(vaug_ref.shape[0]):
        for t in range(v_t.shape[0] // LANES):
            tile = v_t[t * LANES:(t + 1) * LANES, j * MOBA_BLOCK:(j + 1) * MOBA_BLOCK]
            vaug_ref[j, t * PAIR_ROWS:t * PAIR_ROWS + LANES, :] = tile.astype(BF16)
            vaug_ref[j, t * PAIR_ROWS + LANES:(t + 1) * PAIR_ROWS, :] = ones


def _even_in_body(x_ref, g_ref, w_ref, qg_ref, kg_ref, vn_ref, e_ref, *rest, prompt, seq_len, n_alias):
    outs = rest[n_alias:]
    q_ref, k_ref, v_ref, u_ref, vg_ref = outs[:5]
    half = q_ref.shape[-1]
    h = _rms(x_ref[...], g_ref[...]).astype(BF16)
    z = _dot(h, w_ref[...])
    e = e_ref[...]
    q = _head_norm(z[:, 0:half], e, qg_ref[...])
    k = _head_norm(z[:, half:2 * half], e, kg_ref[...])
    v = z[:, 2 * half:3 * half]
    q_ref[...] = q
    if prompt:
        v_t = v.T
        k_ref[...] = k.T
        v_ref[...] = v_t
    else:
        k_ref[...] = k
        v_ref[...] = v
    u_ref[...] = jax.nn.gelu(z[:, 3 * half:4 * half])
    vg_ref[...] = _head_norm(jax.nn.gelu(z[:, 4 * half:5 * half]), e, vn_ref[...])
    if prompt:
        kaug_ref, vaug_ref, km_ref = outs[5:]
        tm = x_ref.shape[0]
        first_pos = lax.rem(pl.program_id(0) * tm, seq_len)
        onehot = _block_onehot(tm, first_pos, seq_len // MOBA_BLOCK).astype(BF16)
        kb = k.astype(BF16)
        for t in range(half // LANES):
            kaug_ref[:, 2 * t * LANES:(2 * t + 1) * LANES] = kb[:, t * LANES:(t + 1) * LANES]
            kaug_ref[:, (2 * t + 1) * LANES:(2 * t + 2) * LANES] = onehot
        _store_value_tiles(vaug_ref, v_t)
        for b in range(tm // MOBA_BLOCK):
            km_ref[b:b + 1, :] = jnp.mean(k[b * MOBA_BLOCK:(b + 1) * MOBA_BLOCK, :], axis=0, keepdims=True)


def _value_tiles_out(t, tm, half):
    per_step = tm // MOBA_BLOCK
    rows = half // LANES * PAIR_ROWS
    spec = pl.BlockSpec((per_step, rows, MOBA_BLOCK), lambda i: (i, 0, 0))
    return spec, jax.ShapeDtypeStruct((t // MOBA_BLOCK, rows, MOBA_BLOCK), BF16)


def _stacked_kv(stacked, n_slots, batch, width, seq_len, slot, tm, first_alias_input):
    tiles_per_seq = seq_len // tm
    spec = pl.BlockSpec((None, None, width, tm), lambda i: (i // tiles_per_seq, slot, 0, i % tiles_per_seq))
    shape = jax.ShapeDtypeStruct((batch, n_slots, width, seq_len), F32)
    if stacked is None:
        return spec, shape, [], [], {}
    any_spec = pl.BlockSpec(memory_space=pl.ANY)
    return spec, shape, [any_spec, any_spec], list(stacked), {first_alias_input: 1, first_alias_input + 1: 2}


def _even_in(x, g, w, qg, kg, vn, e, layer, ev, prompt, seq_len, stacked=None, n_slots=1):
    t, d = x.shape
    n_in = w.shape[-1]
    half = n_in // 5
    tm = _pick(t, (512, 256) if prompt else TOKEN_TILES)
    row = lambda width: pl.BlockSpec((tm, width), lambda i: (i, 0))
    vec = lambda arr: pl.BlockSpec((None, 1, arr.shape[-1]), lambda i: (ev, 0, 0))
    in_specs = [row(d),
                pl.BlockSpec((None, 1, d), lambda i: (layer, 0, 0)),
                pl.BlockSpec((None, d, n_in), lambda i: (ev, 0, 0)),
                vec(qg), vec(kg), vec(vn),
                pl.BlockSpec(e.shape, lambda i: (0, 0))]
    args = [x, g, w, qg, kg, vn, e]
    out_specs = [row(half)] * 5
    out_shape = [jax.ShapeDtypeStruct((t, half), F32)] * 5
    aliases = {}
    if prompt:
        kv_spec, kv_shape, alias_specs, alias_args, aliases = _stacked_kv(
            stacked, n_slots, t // seq_len, half, seq_len, ev, tm, len(in_specs))
        in_specs += alias_specs
        args += alias_args
        nkm = tm // MOBA_BLOCK
        vaug_spec, vaug_shape = _value_tiles_out(t, tm, half)
        out_specs = [row(half), kv_spec, kv_spec, row(half), row(half),
                     row(2 * half), vaug_spec, pl.BlockSpec((None, nkm, half), lambda i: (i, 0, 0))]
        out_shape = [out_shape[0], kv_shape, kv_shape, out_shape[0], out_shape[0],
                     jax.ShapeDtypeStruct((t, 2 * half), BF16), vaug_shape,
                     jax.ShapeDtypeStruct((t // tm, nkm, half), F32)]
    return pl.pallas_call(
        functools.partial(_even_in_body, prompt=prompt, seq_len=seq_len, n_alias=len(aliases)),
        grid=(t // tm,),
        in_specs=in_specs,
        out_specs=out_specs,
        out_shape=out_shape,
        input_output_aliases=aliases,
        compiler_params=_params("parallel"),
        name="even_in",
    )(*args)


def _odd_in_prompt_body(x_ref, g_ref, w_ref, wf_ref, bf_ref, qg_ref, kg_ref, cw_ref, e_ref, p_ref,
                        *rest, seq_len, n_alias):
    q_ref, k_ref, v_ref, yc_ref, lf_ref, kaug_ref, vaug_ref, tail_ref, pre_ref, carry_ref = rest[n_alias:]
    tm = x_ref.shape[0]
    half = q_ref.shape[-1]
    n_heads = lf_ref.shape[-1]
    tiles_per_seq = seq_len // tm
    s_idx = lax.rem(pl.program_id(0), tiles_per_seq)

    h = _rms(x_ref[...], g_ref[...]).astype(BF16)
    z = _dot(h, w_ref[...])
    e = e_ref[...]
    gb = z[:, 0:half]
    pre = z[:, half:2 * half] * z[:, 2 * half:3 * half]
    q_ref[...] = _head_norm(z[:, 3 * half:4 * half], e, qg_ref[...])
    k = _head_norm(z[:, 4 * half:5 * half], e, kg_ref[...])
    v = z[:, 5 * half:6 * half]
    v_t = v.T
    k_ref[...] = k.T
    v_ref[...] = v_t
    _store_value_tiles(vaug_ref, v_t)

    @pl.when(s_idx == 0)
    def _():
        pre_ref[0:SUBLANES, :] = jnp.zeros((SUBLANES, half), F32)
        carry_ref[...] = jnp.zeros_like(carry_ref)

    @pl.when(s_idx != 0)
    def _():
        pre_ref[0:SUBLANES, :] = pre_ref[tm:tm + SUBLANES, :]

    pre_ref[SUBLANES:SUBLANES + tm, :] = pre
    conv = (cw_ref[0:1, :] * pre_ref[SUBLANES - 2:SUBLANES - 2 + tm, :]
            + cw_ref[1:2, :] * pre_ref[SUBLANES - 1:SUBLANES - 1 + tm, :]
            + cw_ref[2:3, :] * pre)
    yc_ref[...] = gb * conv
    tail_ref[...] = pre[tm - SUBLANES:tm, :]

    lane = lax.broadcasted_iota(jnp.int32, (tm, LANES), 1)
    logf = jnp.where(lane < n_heads, _log_sigmoid(_dot(h, wf_ref[...]) + bf_ref[...]), 0.0)
    lf_ref[...] = logf[:, 0:n_heads]
    r_i = lax.broadcasted_iota(jnp.int32, (tm, tm), 0)
    c_i = lax.broadcasted_iota(jnp.int32, (tm, tm), 1)
    tril = jnp.where(c_i <= r_i, 1.0, 0.0).astype(BF16)
    hi, mid, lo = _split3(logf)
    cum = _dot(tril, hi) + _dot(tril, mid) + _dot(tril, lo) + carry_ref[...]
    carry_ref[...] = cum[tm - 1:tm, :]

    dec = _dot(jnp.concatenate(_split3(cum * (-LOG2E)), axis=1), p_ref[...])
    onehot = _block_onehot(tm, s_idx * tm, seq_len // MOBA_BLOCK)
    kb = k.astype(BF16)
    for t in range(half // LANES):
        kaug_ref[:, 2 * t * LANES:(2 * t + 1) * LANES] = kb[:, t * LANES:(t + 1) * LANES]
        kaug_ref[:, (2 * t + 1) * LANES:(2 * t + 2) * LANES] = (
            dec[:, t * LANES:(t + 1) * LANES] + onehot).astype(BF16)


def _odd_in_prompt(x, g, w, wf, bf, qg, kg, cw, e, place, layer, od, seq_len, n_heads, stacked=None, n_slots=1):
    t, d = x.shape
    half = qg.shape[-1]
    tm = _pick(seq_len, (512, 256))
    row = lambda width: pl.BlockSpec((tm, width), lambda i: (i, 0))
    vec = lambda arr: pl.BlockSpec((None,) + arr.shape[1:], lambda i: (od, 0, 0))
    full = lambda arr: pl.BlockSpec(arr.shape, lambda i: (0, 0))
    tiles_per_seq = seq_len // tm
    in_specs = [row(d),
                pl.BlockSpec((None, 1, d), lambda i: (layer, 0, 0)),
                vec(w), vec(wf), vec(bf), vec(qg), vec(kg), vec(cw), full(e), full(place)]
    kv_spec, kv_shape, alias_specs, alias_args, aliases = _stacked_kv(
        stacked, n_slots, t // seq_len, half, seq_len, od, tm, len(in_specs))
    vaug_spec, vaug_shape = _value_tiles_out(t, tm, half)
    return pl.pallas_call(
        functools.partial(_odd_in_prompt_body, seq_len=seq_len, n_alias=len(aliases)),
        grid=(t // tm,),
        in_specs=in_specs + alias_specs,
        out_specs=[row(half), kv_spec, kv_spec, row(half), row(n_heads), row(2 * half), vaug_spec,
                   pl.BlockSpec((None, SUBLANES, half), lambda i: (i // tiles_per_seq, 0, 0))],
        out_shape=[jax.ShapeDtypeStruct((t, half), F32), kv_shape, kv_shape, jax.ShapeDtypeStruct((t, half), F32),
                   jax.ShapeDtypeStruct((t, n_heads), F32),
                   jax.ShapeDtypeStruct((t, 2 * half), BF16), vaug_shape,
                   jax.ShapeDtypeStruct((t // seq_len, SUBLANES, half), F32)],
        scratch_shapes=[pltpu.VMEM((tm + 2 * SUBLANES, half), F32), pltpu.VMEM((1, LANES), F32)],
        input_output_aliases=aliases,
        compiler_params=_params("arbitrary"),
        name="odd_in_prompt",
    )(x, g, w, wf, bf, qg, kg, cw, e, place, *alias_args)


def _odd_in_sample_body(x_ref, g_ref, w_ref, wf_ref, bf_ref, qg_ref, kg_ref, cw_ref, e_ref,
                        b0_ref, b1_ref, yc_ref, q_ref, k_ref, v_ref, lf_ref, pre_ref):
    half = q_ref.shape[-1]
    h = _rms(x_ref[...], g_ref[...]).astype(BF16)
    z = _dot(h, w_ref[...])
    e = e_ref[...]
    pre = z[:, half:2 * half] * z[:, 2 * half:3 * half]
    conv = cw_ref[0:1, :] * b0_ref[...] + cw_ref[1:2, :] * b1_ref[...] + cw_ref[2:3, :] * pre
    yc_ref[...] = z[:, 0:half] * conv
    pre_ref[...] = pre
    q_ref[...] = _head_norm(z[:, 3 * half:4 * half], e, qg_ref[...])
    k_ref[...] = _head_norm(z[:, 4 * half:5 * half], e, kg_ref[...])
    v_ref[...] = z[:, 5 * half:6 * half]
    lf_ref[...] = _log_sigmoid(_dot(h, wf_ref[...]) + bf_ref[...])


def _odd_in_sample(x, g, w, wf, bf, qg, kg, cw, e, b0, b1, layer, od):
    t, d = x.shape
    half = qg.shape[-1]
    tm = _pick(t, TOKEN_TILES)
    row = lambda width: pl.BlockSpec((tm, width), lambda i: (i, 0))
    vec = lambda arr: pl.BlockSpec((None,) + arr.shape[1:], lambda i: (od, 0, 0))
    return pl.pallas_call(
        _odd_in_sample_body,
        grid=(t // tm,),
        in_specs=[row(d),
                  pl.BlockSpec((None, 1, d), lambda i: (layer, 0, 0)),
                  vec(w), vec(wf), vec(bf), vec(qg), vec(kg), vec(cw),
                  pl.BlockSpec(e.shape, lambda i: (0, 0)), row(half), row(half)],
        out_specs=[row(half)] * 4 + [row(LANES), row(half)],
        out_shape=[jax.ShapeDtypeStruct((t, half), F32)] * 4
        + [jax.ShapeDtypeStruct((t, LANES), F32), jax.ShapeDtypeStruct((t, half), F32)],
        compiler_params=_params("parallel"),
        name="odd_in_sample",
    )(x, g, w, wf, bf, qg, kg, cw, e, b0, b1)


def _flash_body(q_ref, kaug_ref, v_ref, *rest, moba, nb, tk, n_tiles):
    if moba:
        km_ref, o_ref, qa_ref = rest
    else:
        o_ref, qa_ref = rest
    tq = MOBA_BLOCK
    qi = pl.program_id(2)
    lane = lax.broadcasted_iota(jnp.int32, (tq, LANES), 1)
    is_blk = lane < nb
    past = lane < qi
    low = lane < HEAD_DIM
    nb_rows = -(-nb // SUBLANES) * SUBLANES
    blk_t = lax.broadcasted_iota(jnp.int32, (nb_rows, tq), 0)
    past_t = blk_t < qi
    blk_f = blk_t.astype(F32)
    own_rows = pl.ds(pl.multiple_of(qi * tq, tq), tq)
    r_i = lax.broadcasted_iota(jnp.int32, (tq, tq), 0)
    c_i = lax.broadcasted_iota(jnp.int32, (tq, tq), 1)
    causal = r_i <= c_i
    causal = jnp.concatenate([causal] * HEADS_PER_TILE, axis=1)
    chains = [(t, hh) for t in range(n_tiles) for hh in range(HEADS_PER_TILE)]

    q_heads = [jnp.where(low if hh == 0 else ~low, q_ref[:, t * LANES:(t + 1) * LANES], 0.0) for t, hh in chains]
    if moba:
        gates = [_dot_t(km_ref[0:nb_rows, t * LANES:(t + 1) * LANES], q_heads[c], precision=HIGHEST)
                 for c, (t, hh) in enumerate(chains)]
    qa_own = []
    for c, (t, hh) in enumerate(chains):
        qh = q_heads[c]
        if moba:
            gate = jnp.where(past_t, gates[c], NEG_INF)
            bias_t = jnp.full((nb_rows, tq), NEG_INF, F32)
            for _ in range(MOBA_TOPK):
                top = jnp.max(gate, axis=0, keepdims=True)
                idx = jnp.min(jnp.where(gate == top, blk_f, float(LANES)), axis=0, keepdims=True)
                pick = blk_f == idx
                bias_t = jnp.where(pick, jnp.where(past_t, 0.0, NEG_INF), bias_t)
                gate = jnp.where(pick, -jnp.inf, gate)
            bias = jnp.concatenate([bias_t, jnp.full((LANES - nb_rows, tq), NEG_INF, F32)], axis=0).T
            aux_own = jnp.zeros((tq, LANES), F32)
        else:
            bias = jnp.where(past, 0.0, NEG_INF)
            d0 = DECAY_LANE0 + DECAY_TERMS * hh
            aux_own = jnp.where((lane >= d0) & (lane < d0 + DECAY_TERMS), 1.0, 0.0)
        aux = jnp.where(is_blk, bias, aux_own)
        qb = (qh * (ATTN_SCALE * LOG2E)).astype(BF16)
        qa_ref[t, hh * tq:(hh + 1) * tq, :] = jnp.concatenate([qb, aux.astype(BF16)], axis=1)
        qa_own.append(jnp.concatenate([qb, aux_own.astype(BF16)], axis=1))

    own_scores = [_dot_t(kaug_ref[own_rows, 2 * t * LANES:(2 * t + 2) * LANES],
                         jnp.concatenate(qa_own[HEADS_PER_TILE * t:HEADS_PER_TILE * (t + 1)], axis=0))
                  for t in range(n_tiles)]
    init = []
    for t in range(n_tiles):
        s = jnp.where(causal, own_scores[t], NEG_INF)
        m = jnp.max(s, axis=0, keepdims=True)
        init.append((m, _dot(v_ref[qi, t * PAIR_ROWS:(t + 1) * PAIR_ROWS, :], jnp.exp2(s - m).astype(BF16))))

    blocks_per_step = tk // tq

    def body(g, carry):
        rows = pl.ds(pl.multiple_of(g * tk, tk), tk)
        scores = [_dot_t(kaug_ref[rows, 2 * t * LANES:(2 * t + 2) * LANES], qa_ref[t])
                  for t in range(n_tiles)]
        new = []
        for t in range(n_tiles):
            m, acc = carry[t]
            m_new = jnp.maximum(m, jnp.max(scores[t], axis=0, keepdims=True))
            p = jnp.exp2(scores[t] - m_new).astype(BF16)
            pv = _dot(v_ref[g * blocks_per_step, t * PAIR_ROWS:(t + 1) * PAIR_ROWS, :], p[0:tq, :])
            for j in range(1, blocks_per_step):
                pv = pv + _dot(v_ref[g * blocks_per_step + j, t * PAIR_ROWS:(t + 1) * PAIR_ROWS, :],
                               p[j * tq:(j + 1) * tq, :])
            new.append((m_new, jnp.exp2(m - m_new) * acc + pv))
        return tuple(new)

    n_steps = lax.div(qi + (blocks_per_step - 1), blocks_per_step)
    final = lax.fori_loop(0, n_steps, body, tuple(init))
    for t in range(n_tiles):
        _, acc = final[t]
        inv = 1.0 / acc[LANES:LANES + 1, :]
        out_t = jnp.concatenate([acc[0:HEAD_DIM, 0:tq] * inv[:, 0:tq],
                                 acc[HEAD_DIM:LANES, tq:2 * tq] * inv[:, tq:2 * tq]], axis=0)
        o_ref[:, t * LANES:(t + 1) * LANES] = out_t.T


def _flash(q, kaug, vaug, kmean, batch, seq_len):
    t, width = q.shape
    tq = MOBA_BLOCK
    nq = seq_len // tq
    nb = seq_len // MOBA_BLOCK
    assert nb <= DECAY_LANE0 and DECAY_LANE0 + HEADS_PER_TILE * DECAY_TERMS <= LANES
    tk = _pick(seq_len, (1024, 512, 256))
    n_tiles = width // LANES
    cols = n_tiles * LANES
    moba = kmean is not None
    in_specs = [pl.BlockSpec((tq, cols), lambda b, hp, i: (b * nq + i, hp)),
                pl.BlockSpec((seq_len, 2 * cols), lambda b, hp, i: (b, hp), pipeline_mode=pl.Buffered(1)),
                pl.BlockSpec((nq, n_tiles * PAIR_ROWS, tq), lambda b, hp, i: (b, hp, 0),
                             pipeline_mode=pl.Buffered(1))]
    args = [q, kaug, vaug]
    if moba:
        in_specs.append(pl.BlockSpec((LANES, cols), lambda b, hp, i: (b, hp)))
        args.append(kmean)
    return pl.pallas_call(
        functools.partial(_flash_body, moba=moba, nb=nb, tk=tk, n_tiles=n_tiles),
        grid=(batch, width // cols, nq),
        in_specs=in_specs,
        out_specs=pl.BlockSpec((tq, cols), lambda b, hp, i: (b * nq + i, hp)),
        out_shape=jax.ShapeDtypeStruct((t, width), F32),
        scratch_shapes=[pltpu.VMEM((n_tiles, HEADS_PER_TILE * tq, 2 * LANES), BF16)],
        compiler_params=_params("parallel", "parallel", "arbitrary"),
        name="moba_attn" if moba else "fox_attn",
    )(*args)


def _paged_body(pt_ref, q_ref, kn_ref, vn_ref, *rest, moba, n_pages, n_heads):
    del pt_ref
    if moba:
        k_refs, v_refs, (o_ref,) = rest[:n_pages], rest[n_pages:2 * n_pages], rest[2 * n_pages:]
    else:
        lfn_ref = rest[0]
        k_refs, v_refs = rest[1:1 + n_pages], rest[1 + n_pages:1 + 2 * n_pages]
        f_refs, (o_ref,) = rest[1 + 2 * n_pages:1 + 3 * n_pages], rest[1 + 3 * n_pages:]
    page = k_refs[0].shape[1]
    width = q_ref.shape[-1]
    assert n_heads == SUBLANES
    q = q_ref[...]
    h_i = lax.broadcasted_iota(jnp.int32, (n_heads, width), 0)
    w_i = lax.broadcasted_iota(jnp.int32, (n_heads, width), 1)
    own = lax.div(w_i, HEAD_DIM) == h_i
    q8 = jnp.where(own, q, 0.0)
    h2_i = lax.broadcasted_iota(jnp.int32, (2 * n_heads, width), 0)
    w2_i = lax.broadcasted_iota(jnp.int32, (2 * n_heads, width), 1)
    qs = jnp.where(lax.div(w2_i, HEAD_DIM) == lax.rem(h2_i, n_heads), q, 0.0) * ATTN_SCALE
    qs_hi = qs.astype(BF16).astype(F32)
    q16 = jnp.where(h2_i < n_heads, qs_hi, qs - qs_hi).astype(BF16)
    s = []
    for p in range(n_pages):
        s16 = _dot(q16, k_refs[p][...].astype(BF16))
        s.append(s16[0:n_heads, :] + s16[n_heads:2 * n_heads, :])

    if moba:
        pages_per_blk = MOBA_BLOCK // page
        n_blk = n_pages // pages_per_blk
        assert MOBA_TOPK <= n_blk <= LANES
        lane = lax.broadcasted_iota(jnp.int32, (n_heads, LANES), 1)
        lane_f = lane.astype(F32)
        gate = jnp.full((n_heads, LANES), NEG_INF, F32)
        for b in range(n_blk):
            tot = sum(jnp.sum(s[b * pages_per_blk + j], axis=-1, keepdims=True) for j in range(pages_per_blk))
            gate = jnp.where(lane == b, tot, gate)
        chosen = jnp.zeros((n_heads, LANES), F32)
        for _ in range(MOBA_TOPK):
            top = jnp.max(gate, axis=-1, keepdims=True)
            idx = jnp.min(jnp.where(gate == top, lane_f, float(LANES)), axis=-1, keepdims=True)
            pick = lane_f == idx
            chosen = jnp.where(pick, 1.0, chosen)
            gate = jnp.where(pick, -jnp.inf, gate)
        for p in range(n_pages):
            on = jnp.max(jnp.where(lane == p // pages_per_blk, chosen, 0.0), axis=-1, keepdims=True)
            s[p] = jnp.where(on > 0.0, s[p], NEG_INF)
    else:
        r_i = lax.broadcasted_iota(jnp.int32, (page, page), 0)
        c_i = lax.broadcasted_iota(jnp.int32, (page, page), 1)
        after_in_page = jnp.where(r_i > c_i, 1.0, 0.0).astype(BF16)
        after = lfn_ref[...]
        for p in reversed(range(n_pages)):
            f = f_refs[p][...]
            hi, mid, lo = _split3(f)
            within = _dot(hi, after_in_page) + _dot(mid, after_in_page) + _dot(lo, after_in_page)
            s[p] = s[p] + (within + after)
            after = after + jnp.sum(f, axis=-1, keepdims=True)

    s_self = jnp.sum(q8 * kn_ref[...], axis=-1, keepdims=True) * ATTN_SCALE
    m = s_self
    for p in range(n_pages):
        m = jnp.maximum(m, jnp.max(s[p], axis=-1, keepdims=True))
    p_self = jnp.exp(s_self - m)
    l = p_self
    acc = p_self * vn_ref[...]
    for p in range(n_pages):
        e = jnp.exp(s[p] - m)
        l = l + jnp.sum(e, axis=-1, keepdims=True)
        acc = acc + _dot_t(e.astype(BF16), v_refs[p][...].astype(BF16))
    o_ref[...] = jnp.sum(jnp.where(own, acc * (1.0 / l), 0.0), axis=0, keepdims=True)


def _paged_attn(page_table, q, kn, vn, cache_k, cache_v, layer_idx, lf_new=None, cache_f=None):
    n, _, width = q.shape
    n_pages = page_table.shape[1]
    n_heads = width // HEAD_DIM
    moba = cache_f is None
    one = pl.BlockSpec((None, 1, width), lambda s, pt: (s, 0, 0))

    def paged(arr, p):
        return pl.BlockSpec((None, None) + arr.shape[2:], lambda s, pt: (pt[s, p], layer_idx, 0, 0))

    in_specs = [one, one, one]
    args = [q, kn, vn]
    if not moba:
        in_specs.append(pl.BlockSpec((None, n_heads, 1), lambda s, pt: (s, 0, 0)))
        args.append(lf_new)
    in_specs += [paged(cache_k, p) for p in range(n_pages)] + [paged(cache_v, p) for p in range(n_pages)]
    args += [cache_k] * n_pages + [cache_v] * n_pages
    if not moba:
        in_specs += [paged(cache_f, p) for p in range(n_pages)]
        args += [cache_f] * n_pages
    return pl.pallas_call(
        functools.partial(_paged_body, moba=moba, n_pages=n_pages, n_heads=n_heads),
        grid_spec=pltpu.PrefetchScalarGridSpec(
            num_scalar_prefetch=1,
            grid=(n,),
            in_specs=in_specs,
            out_specs=pl.BlockSpec((None, 1, width), lambda s, pt: (s, 0, 0)),
        ),
        out_shape=jax.ShapeDtypeStruct((n, 1, width), F32),
        compiler_params=_params("arbitrary"),
        name="moba_paged" if moba else "fox_paged",
    )(page_table, *args)


def _head_ones(width):
    i = np.arange(width) // HEAD_DIM
    return jnp.asarray(i[:, None] == i[None, :], dtype=BF16)


def _decay_placement(n_heads):
    place = np.zeros((DECAY_TERMS * LANES, n_heads // HEADS_PER_TILE * LANES), np.float32)
    for c in range(DECAY_TERMS):
        for h in range(n_heads):
            col = (h // HEADS_PER_TILE) * LANES + DECAY_LANE0 + DECAY_TERMS * (h % HEADS_PER_TILE) + c
            place[c * LANES + h, col] = 1.0
    return jnp.asarray(place, dtype=BF16)


def kernel(x_prompt, x_sample, cache_a_k, cache_a_v, cache_d_k, cache_d_v, cache_d_logf, state_c_conv, page_table, p_prompt, p_sample, norm_ffn1, ffn1_wi, ffn1_wo, norm_mix, ev_w_in, a_q_norm, a_k_norm, b_v_norm, b_w_s, b_bias, ev_w_out, od_w_in, od_b_f, c_conv_w, d_q_norm, d_k_norm, od_w_out, norm_ffn2, ffn2_wi, ffn2_wo, norm_ple, ple_w_gate, ple_w_proj):
    bp, s_len, d = x_prompt.shape
    bs, t_len, _ = x_sample.shape
    depth = norm_ffn1.shape[0]
    n_pool, _, page, h_a, hd = cache_a_k.shape
    h_d = cache_d_k.shape[3]
    n_pages = page_table.shape[1]
    d_a, d_d = h_a * hd, h_d * hd
    d_c = c_conv_w.shape[-1]
    assert hd == HEAD_DIM and t_len == 1 and s_len % MOBA_BLOCK == 0 and d_a == d_d == d_c
    assert (n_pages * page) % MOBA_BLOCK == 0 and b_w_s.shape[-1] == GMLP_CHUNK
    tp, ts = bp * s_len, bs * t_len
    nb = s_len // MOBA_BLOCK

    bf = lambda w: w.astype(BF16)
    vec = lambda g: g.reshape(g.shape[0], 1, -1)
    ffn1_wi, ffn1_wo, ffn2_wi, ffn2_wo = bf(ffn1_wi), bf(ffn1_wo), bf(ffn2_wi), bf(ffn2_wo)
    ple_wg, ple_wp = bf(ple_w_gate), bf(ple_w_proj)
    ev_w, ev_wo, od_wo = bf(ev_w_in), bf(ev_w_out), bf(od_w_out)
    n_main = 3 * d_c + 3 * d_d
    od_w = bf(od_w_in[:, :, :n_main])
    od_wf = od_w_in[:, :, n_main:]
    od_wf = bf(jnp.pad(od_wf, ((0, 0), (0, 0), (0, LANES - h_d))))
    od_bf = jnp.pad(od_b_f, ((0, 0), (0, LANES - h_d)))[:, None, :]
    g_ffn1, g_mix, g_ffn2, g_ple = vec(norm_ffn1), vec(norm_mix), vec(norm_ffn2), vec(norm_ple)
    a_qg = jnp.tile(a_q_norm, (1, h_a))[:, None, :]
    a_kg = jnp.tile(a_k_norm, (1, h_a))[:, None, :]
    b_vn = vec(b_v_norm)
    d_qg = jnp.tile(d_q_norm, (1, h_d))[:, None, :]
    d_kg = jnp.tile(d_k_norm, (1, h_d))[:, None, :]
    conv_w = jnp.pad(c_conv_w, ((0, 0), (0, SUBLANES - CONV_W), (0, 0)))
    gate_bias = jnp.repeat(jnp.swapaxes(b_bias, 1, 2), HEAD_DIM, axis=2)
    gate_w1 = jnp.repeat(b_w_s[:, :, 0, 0], HEAD_DIM, axis=1)[:, None, :]
    gate_b1 = jnp.repeat(b_bias[:, :, 0], HEAD_DIM, axis=1)[:, None, :]
    e_ones = _head_ones(d_a)
    place = _decay_placement(h_d)

    pos_minor = lambda c: jnp.transpose(c, (0, 1, 3, 4, 2)).reshape(n_pool, c.shape[1], -1, page)
    ck_a, cv_a, ck_d, cv_d = pos_minor(cache_a_k), pos_minor(cache_a_v), pos_minor(cache_d_k), pos_minor(cache_d_v)
    cf_d = jnp.swapaxes(cache_d_logf, 2, 3)
    pp = p_prompt.reshape(depth, tp, -1)
    ps = p_sample.reshape(depth, ts, -1)

    xp = x_prompt.reshape(tp, d)
    xs = x_sample.reshape(ts, d)
    n_even, n_odd = (depth + 1) // 2, depth // 2
    a_kv = d_kv = None
    ak_s, av_s, bv_s = [], [], []
    cv_p, cv_s = [], []
    df_p, dk_s, dv_s, df_s = [], [], [], []

    for l in range(depth):
        xp = _ffn_half(xp, g_ffn1, ffn1_wi, ffn1_wo, l)
        xs = _ffn_half(xs, g_ffn1, ffn1_wi, ffn1_wo, l)
        if l % 2 == 0:
            e = l // 2
            q, kt, vt, u, vg, kaug, vaug, km = _even_in(
                xp, g_mix, ev_w, a_qg, a_kg, b_vn, e_ones, l, e, True, s_len, stacked=a_kv, n_slots=n_even)
            a_kv = (kt, vt)
            km = jnp.pad(km.reshape(bp, nb, d_a), ((0, 0), (0, LANES - nb), (0, 0))).reshape(bp * LANES, d_a)
            att = _flash(q, kaug, vaug, km, bp, s_len)
            sg = _gmlp_gate(u, vg, b_w_s, gate_bias, e)
            xp = _mix_out(xp, (att, sg), ev_wo, e)

            q, k, v, u, vg = _even_in(xs, g_mix, ev_w, a_qg, a_kg, b_vn, e_ones, l, e, False, s_len)
            r3 = lambda a: a.reshape(bs, 1, -1)
            att = _paged_attn(page_table, r3(q), r3(k), r3(v), ck_a, cv_a, e).reshape(ts, d_a)
            xs = _mix_out(xs, (att, u, vg), ev_wo, e, gate_vecs=(gate_w1[e], gate_b1[e]))
            ak_s.append(k.reshape(bs, t_len, h_a, hd))
            av_s.append(v.reshape(bs, t_len, h_a, hd))
            bv_s.append(vg.reshape(bs, t_len, -1))
        else:
            o = l // 2
            q, kt, vt, yc, lf, kaug, vaug, tail = _odd_in_prompt(
                xp, g_mix, od_w, od_wf, od_bf, d_qg, d_kg, conv_w, e_ones, place, l, o, s_len, h_d,
                stacked=d_kv, n_slots=n_odd)
            d_kv = (kt, vt)
            att = _flash(q, kaug, vaug, None, bp, s_len)
            xp = _mix_out(xp, (yc, att), od_wo, o)
            cv_p.append(tail[:, SUBLANES - (CONV_W - 1):, :])
            df_p.append(lf.reshape(bp, s_len, h_d))

            buf = state_c_conv[:, o]
            yc, q, k, v, lf, pre = _odd_in_sample(
                xs, g_mix, od_w, od_wf, od_bf, d_qg, d_kg, conv_w, e_ones,
                buf[:, 0], buf[:, 1], l, o)
            r3 = lambda a: a.reshape(bs, 1, -1)
            lf = lf[:, :h_d]
            att = _paged_attn(page_table, r3(q), r3(k), r3(v), ck_d, cv_d, o,
                              lf_new=lf.reshape(bs, h_d, 1), cache_f=cf_d).reshape(ts, d_d)
            xs = _mix_out(xs, (yc, att), od_wo, o)
            cv_s.append(jnp.stack([buf[:, 1], pre], axis=1))
            dk_s.append(k.reshape(bs, t_len, h_d, hd))
            dv_s.append(v.reshape(bs, t_len, h_d, hd))
            df_s.append(lf.reshape(bs, t_len, h_d))
        xp = _ffn_half(xp, g_ffn2, ffn2_wi, ffn2_wo, l)
        xs = _ffn_half(xs, g_ffn2, ffn2_wi, ffn2_wo, l)
        xp = _ple_add(xp, pp, g_ple, ple_wg, ple_wp, l)
        xs = _ple_add(xs, ps, g_ple, ple_wg, ple_wp, l)

    st = lambda rows: jnp.stack(rows, axis=1)
    seq_major = lambda a, heads: jnp.transpose(a.reshape(bp, a.shape[1], heads, hd, s_len), (0, 1, 4, 2, 3))
    return (xp.reshape(bp, s_len, d), xs.reshape(bs, t_len, d),
            seq_major(a_kv[0], h_a), seq_major(a_kv[1], h_a), st(ak_s), st(av_s), st(bv_s),
            st(cv_p), st(cv_s),
            seq_major(d_kv[0], h_d), seq_major(d_kv[1], h_d), st(df_p), st(dk_s), st(dv_s), st(df_s))
```

```python
import functools
import math

import numpy as np
import jax
import jax.numpy as jnp
from jax import lax
from jax.experimental import pallas as pl
from jax.experimental.pallas import tpu as pltpu

F32 = jnp.float32
BF16 = jnp.bfloat16

HEAD_DIM = 64
MOBA_BLOCK = 256
MOBA_TOPK = 3
GMLP_CHUNK = 128
CONV_W = 3
RMS_EPS = 1e-6
NEG_INF = -1e30
ATTN_SCALE = 1.0 / math.sqrt(HEAD_DIM)
LOG2E = math.log2(math.e)

LANES = 128
SUBLANES = 8
MXU_WIDTH = 256
HEADS_PER_TILE = LANES // HEAD_DIM
DECAY_LANE0 = 96
DECAY_TERMS = 3
PAIR_ROWS = LANES + 16
VMEM_LIMIT = 56 * 1024 * 1024
TOKEN_TILES = (512, 256, 128, 64, 32, 16, 8)
HIGHEST = lax.Precision.HIGHEST


def _params(*sem):
    return pltpu.CompilerParams(dimension_semantics=sem, vmem_limit_bytes=VMEM_LIMIT)


def _pick(n, options):
    for t in options:
        if n % t == 0:
            return t
    raise ValueError(f"no tile in {options} divides {n}")


def _rms(x, g):
    return x * lax.rsqrt(jnp.mean(x * x, axis=-1, keepdims=True) + RMS_EPS) * g


def _dot(a, b):
    return jnp.dot(a, b, preferred_element_type=F32)


def _dot_t(a, b, precision=None):
    return lax.dot_general(a, b, (((1,), (1,)), ((), ())), precision=precision,
                           preferred_element_type=F32)


def _split3(x):
    hi = x.astype(BF16)
    r = x - hi.astype(F32)
    mid = r.astype(BF16)
    lo = (r - mid.astype(F32)).astype(BF16)
    return hi, mid, lo


def _log_sigmoid(x):
    return jnp.minimum(x, 0.0) - jnp.log1p(jnp.exp(-jnp.abs(x)))


def _head_norm(t, e, gain):
    ss = _dot((t * t).astype(BF16), e)
    return t * lax.rsqrt(ss * (1.0 / HEAD_DIM) + RMS_EPS) * gain


def _swiglu(x, g, wi_ref, wo_ref, chunks):
    f = wo_ref.shape[0]
    hn = _rms(x, g).astype(BF16)

    def gate_up(c0, c1):
        return _dot(hn, wi_ref[:, c0:c1]), _dot(hn, wi_ref[:, f + c0:f + c1])

    nxt = gate_up(*chunks[0])
    acc = None
    for i, (c0, c1) in enumerate(chunks):
        a, b = nxt
        if i + 1 < len(chunks):
            nxt = gate_up(*chunks[i + 1])
        part = _dot((jax.nn.silu(a) * b).astype(BF16), wo_ref[c0:c1, :])
        acc = part if acc is None else acc + part
    return acc


def _ffn_chunks(f):
    step = 2 * MXU_WIDTH
    chunks = tuple((c0, min(c0 + step, f)) for c0 in range(0, f, step))
    assert all((c1 - c0) % MXU_WIDTH == 0 for c0, c1 in chunks)
    return chunks


RESIDENT = dict(pipeline_mode=pl.Buffered(1))


def _ffn_body(x_ref, g_ref, wi_ref, wo_ref, o_ref, *, chunks):
    x = x_ref[...]
    o_ref[...] = x + 0.5 * _swiglu(x, g_ref[...], wi_ref, wo_ref, chunks)


def _ffn_half(x, g, wi, wo, layer):
    t, d = x.shape
    f = wo.shape[1]
    tm = _pick(t, TOKEN_TILES)
    return pl.pallas_call(
        functools.partial(_ffn_body, chunks=_ffn_chunks(f)),
        grid=(t // tm,),
        in_specs=[
            pl.BlockSpec((tm, d), lambda i: (i, 0)),
            pl.BlockSpec((None, 1, d), lambda i: (layer, 0, 0)),
            pl.BlockSpec((None, d, 2 * f), lambda i: (layer, 0, 0), **RESIDENT),
            pl.BlockSpec((None, f, d), lambda i: (layer, 0, 0), **RESIDENT),
        ],
        out_specs=pl.BlockSpec((tm, d), lambda i: (i, 0)),
        out_shape=jax.ShapeDtypeStruct((t, d), F32),
        compiler_params=_params("parallel"),
        name="ffn_half",
    )(x, g, wi, wo)


def _tail_body(*refs, chunks, gated):
    if gated:
        x_ref, a_ref, u_ref, vg_ref, wv_ref, bv_ref = refs[:6]
        b = u_ref[...] * (wv_ref[...] * vg_ref[...] + bv_ref[...])
        rest = refs[6:]
    else:
        x_ref, a_ref, b_ref = refs[:3]
        b = b_ref[...]
        rest = refs[3:]
    wm_ref, g2_ref, wi_ref, wo_ref, p_ref, gp_ref, wg_ref, wp_ref, o_ref = rest
    half = a_ref.shape[-1]
    proj = _dot(p_ref[...].astype(BF16), wp_ref[...])
    x = (x_ref[...] + _dot(a_ref[...].astype(BF16), wm_ref[0:half, :])
         + _dot(b.astype(BF16), wm_ref[half:2 * half, :]))
    x = x + 0.5 * _swiglu(x, g2_ref[...], wi_ref, wo_ref, chunks)
    gate = jax.nn.sigmoid(_dot(_rms(x, gp_ref[...]).astype(BF16), wg_ref[...]))
    o_ref[...] = x + gate * proj


def _layer_tail(x, parts, wm, mixer, g2, wi, wo, p, gp, wg, wp, layer, gate_vecs=None):
    t, d = x.shape
    half = parts[0].shape[-1]
    f = wo.shape[1]
    pdim = p.shape[-1]
    tm = _pick(t, TOKEN_TILES)
    row = lambda width: pl.BlockSpec((tm, width), lambda i: (i, 0))
    vec = pl.BlockSpec((1, half), lambda i: (0, 0))
    norm = pl.BlockSpec((None, 1, d), lambda i: (layer, 0, 0))
    in_specs = [row(d)] + [row(half)] * len(parts)
    args = [x, *parts]
    if gate_vecs is not None:
        in_specs += [vec, vec]
        args += list(gate_vecs)
    in_specs += [pl.BlockSpec((None, 2 * half, d), lambda i: (mixer, 0, 0), **RESIDENT),
                 norm,
                 pl.BlockSpec((None, d, 2 * f), lambda i: (layer, 0, 0), **RESIDENT),
                 pl.BlockSpec((None, f, d), lambda i: (layer, 0, 0), **RESIDENT),
                 pl.BlockSpec((None, tm, pdim), lambda i: (layer, i, 0)),
                 norm,
                 pl.BlockSpec((None, d, d), lambda i: (layer, 0, 0), **RESIDENT),
                 pl.BlockSpec((None, pdim, d), lambda i: (layer, 0, 0), **RESIDENT)]
    args += [wm, g2, wi, wo, p, gp, wg, wp]
    return pl.pallas_call(
        functools.partial(_tail_body, chunks=_ffn_chunks(f), gated=gate_vecs is not None),
        grid=(t // tm,),
        in_specs=in_specs,
        out_specs=row(d),
        out_shape=jax.ShapeDtypeStruct((t, d), F32),
        compiler_params=_params("parallel"),
        name="layer_tail",
    )(*args)


def _gmlp_body(u_ref, vg_ref, ws_ref, bias_ref, o_ref):
    tm = u_ref.shape[0]
    c = GMLP_CHUNK
    r_i = lax.broadcasted_iota(jnp.int32, (c, c), 0)
    c_i = lax.broadcasted_iota(jnp.int32, (c, c), 1)
    tril = c_i <= r_i
    first_head = c_i < HEAD_DIM
    n_groups = ws_ref.shape[0]
    w = [jnp.where(tril, ws_ref[g], 0.0).astype(BF16) for g in range(n_groups)]
    bias = bias_ref[...]
    for ci in range(tm // c):
        rows = pl.ds(ci * c, c)
        vg = vg_ref[rows, :].astype(BF16)
        tiles = []
        for t in range(n_groups // HEADS_PER_TILE):
            vt = vg[:, t * LANES:(t + 1) * LANES]
            tiles.append(jnp.where(first_head, _dot(w[2 * t], vt), _dot(w[2 * t + 1], vt)))
        mixed = jnp.concatenate(tiles, axis=1) + bias
        o_ref[rows, :] = u_ref[rows, :] * mixed


def _gmlp_gate(u, vg, ws, bias, layer):
    t, width = u.shape
    tm = _pick(t, TOKEN_TILES)
    g, c = ws.shape[1], ws.shape[2]
    row = pl.BlockSpec((tm, width), lambda i: (i, 0))
    return pl.pallas_call(
        _gmlp_body,
        grid=(t // tm,),
        in_specs=[row, row,
                  pl.BlockSpec((None, g, c, c), lambda i: (layer, 0, 0, 0)),
                  pl.BlockSpec((None, c, width), lambda i: (layer, 0, 0))],
        out_specs=row,
        out_shape=jax.ShapeDtypeStruct((t, width), F32),
        compiler_params=_params("parallel"),
        name="gmlp_gate",
    )(u, vg, ws, bias)


def _block_onehot(tile_rows, first_pos, nb):
    r = lax.broadcasted_iota(jnp.int32, (tile_rows, LANES), 0)
    c = lax.broadcasted_iota(jnp.int32, (tile_rows, LANES), 1)
    blk = lax.div(first_pos + r, MOBA_BLOCK)
    return jnp.where((c == blk) & (c < nb), 1.0, 0.0)


def _store_value_tiles(vaug_ref, v_t):
    ones = jnp.ones((PAIR_ROWS - LANES, MOBA_BLOCK), BF16)
    for j in range(vaug_ref.shape[0]):
        for t in range(v_t.shape[0] // LANES):
            tile = v_t[t * LANES:(t + 1) * LANES, j * MOBA_BLOCK:(j + 1) * MOBA_BLOCK]
            vaug_ref[j, t * PAIR_ROWS:t * PAIR_ROWS + LANES, :] = tile.astype(BF16)
            vaug_ref[j, t * PAIR_ROWS + LANES:(t + 1) * PAIR_ROWS, :] = ones


def _even_in_body(x_ref, g_ref, w_ref, qg_ref, kg_ref, vn_ref, e_ref, *rest, prompt, seq_len, n_alias):
    outs = rest[n_alias:]
    q_ref, k_ref, v_ref, u_ref, vg_ref = outs[:5]
    half = q_ref.shape[-1]
    h = _rms(x_ref[...], g_ref[...]).astype(BF16)
    e = e_ref[...]
    part = lambda n: _dot(h, w_ref[:, n * half:(n + 1) * half])
    z_q, z_k = part(0), part(1)
    q_ref[...] = _head_norm(z_q, e, qg_ref[...])
    z_vg = part(4)
    k = _head_norm(z_k, e, kg_ref[...])
    z_u = part(3)
    vg_ref[...] = _head_norm(jax.nn.gelu(z_vg), e, vn_ref[...])
    v = part(2)
    u_ref[...] = jax.nn.gelu(z_u)
    if prompt:
        v_t = v.T
        k_ref[...] = k.T
        v_ref[...] = v_t
    else:
        k_ref[...] = k
        v_ref[...] = v
    if prompt:
        kaug_ref, vaug_ref, km_ref = outs[5:]
        tm = x_ref.shape[0]
        first_pos = lax.rem(pl.program_id(0) * tm, seq_len)
        onehot = _block_onehot(tm, first_pos, seq_len // MOBA_BLOCK).astype(BF16)
        kb = k.astype(BF16)
        for t in range(half // LANES):
            kaug_ref[:, 2 * t * LANES:(2 * t + 1) * LANES] = kb[:, t * LANES:(t + 1) * LANES]
            kaug_ref[:, (2 * t + 1) * LANES:(2 * t + 2) * LANES] = onehot
        _store_value_tiles(vaug_ref, v_t)
        for b in range(tm // MOBA_BLOCK):
            km_ref[b:b + 1, :] = jnp.mean(k[b * MOBA_BLOCK:(b + 1) * MOBA_BLOCK, :], axis=0, keepdims=True)


def _value_tiles_out(t, tm, half):
    per_step = tm // MOBA_BLOCK
    rows = half // LANES * PAIR_ROWS
    spec = pl.BlockSpec((per_step, rows, MOBA_BLOCK), lambda i: (i, 0, 0))
    return spec, jax.ShapeDtypeStruct((t // MOBA_BLOCK, rows, MOBA_BLOCK), BF16)


def _stacked_kv(stacked, n_slots, batch, width, seq_len, slot, tm, first_alias_input):
    tiles_per_seq = seq_len // tm
    spec = pl.BlockSpec((None, None, width, tm), lambda i: (i // tiles_per_seq, slot, 0, i % tiles_per_seq))
    shape = jax.ShapeDtypeStruct((batch, n_slots, width, seq_len), F32)
    if stacked is None:
        return spec, shape, [], [], {}
    any_spec = pl.BlockSpec(memory_space=pl.ANY)
    return spec, shape, [any_spec, any_spec], list(stacked), {first_alias_input: 1, first_alias_input + 1: 2}


def _even_in(x, g, w, qg, kg, vn, e, layer, ev, prompt, seq_len, stacked=None, n_slots=1):
    t, d = x.shape
    n_in = w.shape[-1]
    half = n_in // 5
    tm = _pick(t, (512, 256) if prompt else TOKEN_TILES)
    row = lambda width: pl.BlockSpec((tm, width), lambda i: (i, 0))
    vec = lambda arr: pl.BlockSpec((None, 1, arr.shape[-1]), lambda i: (ev, 0, 0))
    in_specs = [row(d),
                pl.BlockSpec((None, 1, d), lambda i: (layer, 0, 0)),
                pl.BlockSpec((None, d, n_in), lambda i: (ev, 0, 0)),
                vec(qg), vec(kg), vec(vn),
                pl.BlockSpec(e.shape, lambda i: (0, 0))]
    args = [x, g, w, qg, kg, vn, e]
    out_specs = [row(half)] * 5
    out_shape = [jax.ShapeDtypeStruct((t, half), F32)] * 5
    aliases = {}
    if prompt:
        kv_spec, kv_shape, alias_specs, alias_args, aliases = _stacked_kv(
            stacked, n_slots, t // seq_len, half, seq_len, ev, tm, len(in_specs))
        in_specs += alias_specs
        args += alias_args
        nkm = tm // MOBA_BLOCK
        vaug_spec, vaug_shape = _value_tiles_out(t, tm, half)
        out_specs = [row(half), kv_spec, kv_spec, row(half), row(half),
                     row(2 * half), vaug_spec, pl.BlockSpec((None, nkm, half), lambda i: (i, 0, 0))]
        out_shape = [out_shape[0], kv_shape, kv_shape, out_shape[0], out_shape[0],
                     jax.ShapeDtypeStruct((t, 2 * half), BF16), vaug_shape,
                     jax.ShapeDtypeStruct((t // tm, nkm, half), F32)]
    return pl.pallas_call(
        functools.partial(_even_in_body, prompt=prompt, seq_len=seq_len, n_alias=len(aliases)),
        grid=(t // tm,),
        in_specs=in_specs,
        out_specs=out_specs,
        out_shape=out_shape,
        input_output_aliases=aliases,
        compiler_params=_params("parallel"),
        name="even_in",
    )(*args)


def _odd_in_prompt_body(x_ref, g_ref, w_ref, wf_ref, bf_ref, qg_ref, kg_ref, cw_ref, e_ref, p_ref,
                        *rest, seq_len, n_alias):
    q_ref, k_ref, v_ref, yc_ref, lf_ref, kaug_ref, vaug_ref, tail_ref, pre_ref, carry_ref = rest[n_alias:]
    tm = x_ref.shape[0]
    half = q_ref.shape[-1]
    n_heads = lf_ref.shape[-1]
    tiles_per_seq = seq_len // tm
    s_idx = lax.rem(pl.program_id(0), tiles_per_seq)

    @pl.when(s_idx == 0)
    def _():
        pre_ref[0:SUBLANES, :] = jnp.zeros((SUBLANES, half), F32)
        carry_ref[...] = jnp.zeros_like(carry_ref)

    @pl.when(s_idx != 0)
    def _():
        pre_ref[0:SUBLANES, :] = pre_ref[tm:tm + SUBLANES, :]

    h = _rms(x_ref[...], g_ref[...]).astype(BF16)
    e = e_ref[...]
    part = lambda n: _dot(h, w_ref[:, n * half:(n + 1) * half])
    f_logit = _dot(h, wf_ref[...])
    z_gc, z_hc = part(1), part(2)

    lane = lax.broadcasted_iota(jnp.int32, (tm, LANES), 1)
    logf = jnp.where(lane < n_heads, _log_sigmoid(f_logit + bf_ref[...]), 0.0)
    lf_ref[...] = logf[:, 0:n_heads]
    r_i = lax.broadcasted_iota(jnp.int32, (tm, tm), 0)
    c_i = lax.broadcasted_iota(jnp.int32, (tm, tm), 1)
    tril = jnp.where(c_i <= r_i, 1.0, 0.0).astype(BF16)
    hi, mid, lo = _split3(logf)
    cum = _dot(tril, hi) + _dot(tril, mid) + _dot(tril, lo) + carry_ref[...]
    carry_ref[...] = cum[tm - 1:tm, :]
    gb = part(0)

    pre = z_gc * z_hc
    pre_ref[SUBLANES:SUBLANES + tm, :] = pre
    conv = (cw_ref[0:1, :] * pre_ref[SUBLANES - 2:SUBLANES - 2 + tm, :]
            + cw_ref[1:2, :] * pre_ref[SUBLANES - 1:SUBLANES - 1 + tm, :]
            + cw_ref[2:3, :] * pre)
    z_q = part(3)
    yc_ref[...] = gb * conv
    tail_ref[...] = pre[tm - SUBLANES:tm, :]
    z_k = part(4)
    q_ref[...] = _head_norm(z_q, e, qg_ref[...])
    v = part(5)
    k = _head_norm(z_k, e, kg_ref[...])
    v_t = v.T
    k_ref[...] = k.T
    v_ref[...] = v_t
    _store_value_tiles(vaug_ref, v_t)

    dec = _dot(jnp.concatenate(_split3(cum * (-LOG2E)), axis=1), p_ref[...])
    onehot = _block_onehot(tm, s_idx * tm, seq_len // MOBA_BLOCK)
    kb = k.astype(BF16)
    for t in range(half // LANES):
        kaug_ref[:, 2 * t * LANES:(2 * t + 1) * LANES] = kb[:, t * LANES:(t + 1) * LANES]
        kaug_ref[:, (2 * t + 1) * LANES:(2 * t + 2) * LANES] = (
            dec[:, t * LANES:(t + 1) * LANES] + onehot).astype(BF16)


def _odd_in_prompt(x, g, w, wf, bf, qg, kg, cw, e, place, layer, od, seq_len, n_heads, stacked=None, n_slots=1):
    t, d = x.shape
    half = qg.shape[-1]
    tm = _pick(seq_len, (512, 256))
    row = lambda width: pl.BlockSpec((tm, width), lambda i: (i, 0))
    vec = lambda arr: pl.BlockSpec((None,) + arr.shape[1:], lambda i: (od, 0, 0))
    full = lambda arr: pl.BlockSpec(arr.shape, lambda i: (0, 0))
    tiles_per_seq = seq_len // tm
    in_specs = [row(d),
                pl.BlockSpec((None, 1, d), lambda i: (layer, 0, 0)),
                vec(w), vec(wf), vec(bf), vec(qg), vec(kg), vec(cw), full(e), full(place)]
    kv_spec, kv_shape, alias_specs, alias_args, aliases = _stacked_kv(
        stacked, n_slots, t // seq_len, half, seq_len, od, tm, len(in_specs))
    vaug_spec, vaug_shape = _value_tiles_out(t, tm, half)
    return pl.pallas_call(
        functools.partial(_odd_in_prompt_body, seq_len=seq_len, n_alias=len(aliases)),
        grid=(t // tm,),
        in_specs=in_specs + alias_specs,
        out_specs=[row(half), kv_spec, kv_spec, row(half), row(n_heads), row(2 * half), vaug_spec,
                   pl.BlockSpec((None, SUBLANES, half), lambda i: (i // tiles_per_seq, 0, 0))],
        out_shape=[jax.ShapeDtypeStruct((t, half), F32), kv_shape, kv_shape, jax.ShapeDtypeStruct((t, half), F32),
                   jax.ShapeDtypeStruct((t, n_heads), F32),
                   jax.ShapeDtypeStruct((t, 2 * half), BF16), vaug_shape,
                   jax.ShapeDtypeStruct((t // seq_len, SUBLANES, half), F32)],
        scratch_shapes=[pltpu.VMEM((tm + 2 * SUBLANES, half), F32), pltpu.VMEM((1, LANES), F32)],
        input_output_aliases=aliases,
        compiler_params=_params("arbitrary"),
        name="odd_in_prompt",
    )(x, g, w, wf, bf, qg, kg, cw, e, place, *alias_args)


def _odd_in_sample_body(x_ref, g_ref, w_ref, wf_ref, bf_ref, qg_ref, kg_ref, cw_ref, e_ref,
                        b0_ref, b1_ref, yc_ref, q_ref, k_ref, v_ref, lf_ref, pre_ref):
    half = q_ref.shape[-1]
    h = _rms(x_ref[...], g_ref[...]).astype(BF16)
    z = _dot(h, w_ref[...])
    e = e_ref[...]
    pre = z[:, half:2 * half] * z[:, 2 * half:3 * half]
    conv = cw_ref[0:1, :] * b0_ref[...] + cw_ref[1:2, :] * b1_ref[...] + cw_ref[2:3, :] * pre
    yc_ref[...] = z[:, 0:half] * conv
    pre_ref[...] = pre
    q_ref[...] = _head_norm(z[:, 3 * half:4 * half], e, qg_ref[...])
    k_ref[...] = _head_norm(z[:, 4 * half:5 * half], e, kg_ref[...])
    v_ref[...] = z[:, 5 * half:6 * half]
    lf_ref[...] = _log_sigmoid(_dot(h, wf_ref[...]) + bf_ref[...])


def _odd_in_sample(x, g, w, wf, bf, qg, kg, cw, e, b0, b1, layer, od):
    t, d = x.shape
    half = qg.shape[-1]
    tm = _pick(t, TOKEN_TILES)
    row = lambda width: pl.BlockSpec((tm, width), lambda i: (i, 0))
    vec = lambda arr: pl.BlockSpec((None,) + arr.shape[1:], lambda i: (od, 0, 0))
    return pl.pallas_call(
        _odd_in_sample_body,
        grid=(t // tm,),
        in_specs=[row(d),
                  pl.BlockSpec((None, 1, d), lambda i: (layer, 0, 0)),
                  vec(w), vec(wf), vec(bf), vec(qg), vec(kg), vec(cw),
                  pl.BlockSpec(e.shape, lambda i: (0, 0)), row(half), row(half)],
        out_specs=[row(half)] * 4 + [row(LANES), row(half)],
        out_shape=[jax.ShapeDtypeStruct((t, half), F32)] * 4
        + [jax.ShapeDtypeStruct((t, LANES), F32), jax.ShapeDtypeStruct((t, half), F32)],
        compiler_params=_params("parallel"),
        name="odd_in_sample",
    )(x, g, w, wf, bf, qg, kg, cw, e, b0, b1)


def _flash_body(q_ref, kaug_ref, v_ref, *rest, moba, nb, tk, n_tiles):
    if moba:
        km_ref, o_ref, qa_ref = rest
    else:
        o_ref, qa_ref = rest
    tq = MOBA_BLOCK
    qi = pl.program_id(2)
    lane = lax.broadcasted_iota(jnp.int32, (tq, LANES), 1)
    is_blk = lane < nb
    past = lane < qi
    low = lane < HEAD_DIM
    nb_rows = -(-nb // SUBLANES) * SUBLANES
    blk_t = lax.broadcasted_iota(jnp.int32, (nb_rows, tq), 0)
    past_t = blk_t < qi
    blk_f = blk_t.astype(F32)
    own_rows = pl.ds(pl.multiple_of(qi * tq, tq), tq)
    r_i = lax.broadcasted_iota(jnp.int32, (tq, tq), 0)
    c_i = lax.broadcasted_iota(jnp.int32, (tq, tq), 1)
    causal = r_i <= c_i
    causal = jnp.concatenate([causal] * HEADS_PER_TILE, axis=1)
    chains = [(t, hh) for t in range(n_tiles) for hh in range(HEADS_PER_TILE)]

    q_heads = [jnp.where(low if hh == 0 else ~low, q_ref[:, t * LANES:(t + 1) * LANES], 0.0) for t, hh in chains]
    if moba:
        gates = [_dot_t(km_ref[0:nb_rows, t * LANES:(t + 1) * LANES], q_heads[c], precision=HIGHEST)
                 for c, (t, hh) in enumerate(chains)]
    qa_own = []
    for c, (t, hh) in enumerate(chains):
        qh = q_heads[c]
        if moba:
            gate = jnp.where(past_t, gates[c], NEG_INF)
            bias_t = jnp.full((nb_rows, tq), NEG_INF, F32)
            for _ in range(MOBA_TOPK):
                top = jnp.max(gate, axis=0, keepdims=True)
                idx = jnp.min(jnp.where(gate == top, blk_f, float(LANES)), axis=0, keepdims=True)
                pick = blk_f == idx
                bias_t = jnp.where(pick, jnp.where(past_t, 0.0, NEG_INF), bias_t)
                gate = jnp.where(pick, -jnp.inf, gate)
            bias = jnp.concatenate([bias_t, jnp.full((LANES - nb_rows, tq), NEG_INF, F32)], axis=0).T
            aux_own = jnp.zeros((tq, LANES), F32)
        else:
            bias = jnp.where(past, 0.0, NEG_INF)
            d0 = DECAY_LANE0 + DECAY_TERMS * hh
            aux_own = jnp.where((lane >= d0) & (lane < d0 + DECAY_TERMS), 1.0, 0.0)
        aux = jnp.where(is_blk, bias, aux_own)
        qb = (qh * (ATTN_SCALE * LOG2E)).astype(BF16)
        qa_ref[t, hh * tq:(hh + 1) * tq, :] = jnp.concatenate([qb, aux.astype(BF16)], axis=1)
        qa_own.append(jnp.concatenate([qb, aux_own.astype(BF16)], axis=1))

    own_scores = [_dot_t(kaug_ref[own_rows, 2 * t * LANES:(2 * t + 2) * LANES],
                         jnp.concatenate(qa_own[HEADS_PER_TILE * t:HEADS_PER_TILE * (t + 1)], axis=0))
                  for t in range(n_tiles)]
    init = []
    for t in range(n_tiles):
        s = jnp.where(causal, own_scores[t], NEG_INF)
        m = jnp.max(s, axis=0, keepdims=True)
        init.append((m, _dot(v_ref[qi, t * PAIR_ROWS:(t + 1) * PAIR_ROWS, :], jnp.exp2(s - m).astype(BF16))))

    blocks_per_step = tk // tq

    def body(g, carry):
        rows = pl.ds(pl.multiple_of(g * tk, tk), tk)
        scores = [_dot_t(kaug_ref[rows, 2 * t * LANES:(2 * t + 2) * LANES], qa_ref[t])
                  for t in range(n_tiles)]
        new = []
        for t in range(n_tiles):
            m, acc = carry[t]
            m_new = jnp.maximum(m, jnp.max(scores[t], axis=0, keepdims=True))
            p = jnp.exp2(scores[t] - m_new).astype(BF16)
            pv = _dot(v_ref[g * blocks_per_step, t * PAIR_ROWS:(t + 1) * PAIR_ROWS, :], p[0:tq, :])
            for j in range(1, blocks_per_step):
                pv = pv + _dot(v_ref[g * blocks_per_step + j, t * PAIR_ROWS:(t + 1) * PAIR_ROWS, :],
                               p[j * tq:(j + 1) * tq, :])
            new.append((m_new, jnp.exp2(m - m_new) * acc + pv))
        return tuple(new)

    n_steps = lax.div(qi + (blocks_per_step - 1), blocks_per_step)
    final = lax.fori_loop(0, n_steps, body, tuple(init))
    for t in range(n_tiles):
        _, acc = final[t]
        inv = 1.0 / acc[LANES:LANES + 1, :]
        out_t = jnp.concatenate([acc[0:HEAD_DIM, 0:tq] * inv[:, 0:tq],
                                 acc[HEAD_DIM:LANES, tq:2 * tq] * inv[:, tq:2 * tq]], axis=0)
        o_ref[:, t * LANES:(t + 1) * LANES] = out_t.T


def _flash(q, kaug, vaug, kmean, batch, seq_len):
    t, width = q.shape
    tq = MOBA_BLOCK
    nq = seq_len // tq
    nb = seq_len // MOBA_BLOCK
    assert nb <= DECAY_LANE0 and DECAY_LANE0 + HEADS_PER_TILE * DECAY_TERMS <= LANES
    tk = _pick(seq_len, (1024, 512, 256))
    n_tiles = width // LANES
    cols = n_tiles * LANES
    moba = kmean is not None
    in_specs = [pl.BlockSpec((tq, cols), lambda b, hp, i: (b * nq + i, hp)),
                pl.BlockSpec((seq_len, 2 * cols), lambda b, hp, i: (b, hp), pipeline_mode=pl.Buffered(1)),
                pl.BlockSpec((nq, n_tiles * PAIR_ROWS, tq), lambda b, hp, i: (b, hp, 0),
                             pipeline_mode=pl.Buffered(1))]
    args = [q, kaug, vaug]
    if moba:
        in_specs.append(pl.BlockSpec((LANES, cols), lambda b, hp, i: (b, hp)))
        args.append(kmean)
    return pl.pallas_call(
        functools.partial(_flash_body, moba=moba, nb=nb, tk=tk, n_tiles=n_tiles),
        grid=(batch, width // cols, nq),
        in_specs=in_specs,
        out_specs=pl.BlockSpec((tq, cols), lambda b, hp, i: (b * nq + i, hp)),
        out_shape=jax.ShapeDtypeStruct((t, width), F32),
        scratch_shapes=[pltpu.VMEM((n_tiles, HEADS_PER_TILE * tq, 2 * LANES), BF16)],
        compiler_params=_params("parallel", "parallel", "arbitrary"),
        name="moba_attn" if moba else "fox_attn",
    )(*args)


def _paged_body(pt_ref, q_ref, kn_ref, vn_ref, *rest, moba, n_pages, n_heads):
    del pt_ref
    if moba:
        k_refs, v_refs, (o_ref,) = rest[:n_pages], rest[n_pages:2 * n_pages], rest[2 * n_pages:]
    else:
        lfn_ref = rest[0]
        k_refs, v_refs = rest[1:1 + n_pages], rest[1 + n_pages:1 + 2 * n_pages]
        f_refs, (o_ref,) = rest[1 + 2 * n_pages:1 + 3 * n_pages], rest[1 + 3 * n_pages:]
    page = k_refs[0].shape[1]
    width = q_ref.shape[-1]
    assert n_heads == SUBLANES
    q = q_ref[...]
    h_i = lax.broadcasted_iota(jnp.int32, (n_heads, width), 0)
    w_i = lax.broadcasted_iota(jnp.int32, (n_heads, width), 1)
    own = lax.div(w_i, HEAD_DIM) == h_i
    q8 = jnp.where(own, q, 0.0)
    h2_i = lax.broadcasted_iota(jnp.int32, (2 * n_heads, width), 0)
    w2_i = lax.broadcasted_iota(jnp.int32, (2 * n_heads, width), 1)
    qs = jnp.where(lax.div(w2_i, HEAD_DIM) == lax.rem(h2_i, n_heads), q, 0.0) * ATTN_SCALE
    qs_hi = qs.astype(BF16).astype(F32)
    q16 = jnp.where(h2_i < n_heads, qs_hi, qs - qs_hi).astype(BF16)
    s = []
    for p in range(n_pages):
        s16 = _dot(q16, k_refs[p][...].astype(BF16))
        s.append(s16[0:n_heads, :] + s16[n_heads:2 * n_heads, :])

    if moba:
        pages_per_blk = MOBA_BLOCK // page
        n_blk = n_pages // pages_per_blk
        assert MOBA_TOPK <= n_blk <= LANES
        lane = lax.broadcasted_iota(jnp.int32, (n_heads, LANES), 1)
        lane_f = lane.astype(F32)
        gate = jnp.full((n_heads, LANES), NEG_INF, F32)
        for b in range(n_blk):
            tot = sum(jnp.sum(s[b * pages_per_blk + j], axis=-1, keepdims=True) for j in range(pages_per_blk))
            gate = jnp.where(lane == b, tot, gate)
        chosen = jnp.zeros((n_heads, LANES), F32)
        for _ in range(MOBA_TOPK):
            top = jnp.max(gate, axis=-1, keepdims=True)
            idx = jnp.min(jnp.where(gate == top, lane_f, float(LANES)), axis=-1, keepdims=True)
            pick = lane_f == idx
            chosen = jnp.where(pick, 1.0, chosen)
            gate = jnp.where(pick, -jnp.inf, gate)
        for p in range(n_pages):
            on = jnp.max(jnp.where(lane == p // pages_per_blk, chosen, 0.0), axis=-1, keepdims=True)
            s[p] = jnp.where(on > 0.0, s[p], NEG_INF)
    else:
        r_i = lax.broadcasted_iota(jnp.int32, (page, page), 0)
        c_i = lax.broadcasted_iota(jnp.int32, (page, page), 1)
        after_in_page = jnp.where(r_i > c_i, 1.0, 0.0).astype(BF16)
        after = lfn_ref[...]
        for p in reversed(range(n_pages)):
            f = f_refs[p][...]
            hi, mid, lo = _split3(f)
            within = _dot(hi, after_in_page) + _dot(mid, after_in_page) + _dot(lo, after_in_page)
            s[p] = s[p] + (within + after)
            after = after + jnp.sum(f, axis=-1, keepdims=True)

    s_self = jnp.sum(q8 * kn_ref[...], axis=-1, keepdims=True) * ATTN_SCALE
    m = s_self
    for p in range(n_pages):
        m = jnp.maximum(m, jnp.max(s[p], axis=-1, keepdims=True))
    p_self = jnp.exp(s_self - m)
    l = p_self
    acc = p_self * vn_ref[...]
    for p in range(n_pages):
        e = jnp.exp(s[p] - m)
        l = l + jnp.sum(e, axis=-1, keepdims=True)
        acc = acc + _dot_t(e.astype(BF16), v_refs[p][...].astype(BF16))
    o_ref[...] = jnp.sum(jnp.where(own, acc * (1.0 / l), 0.0), axis=0, keepdims=True)


def _paged_attn(page_table, q, kn, vn, cache_k, cache_v, layer_idx, lf_new=None, cache_f=None):
    n, _, width = q.shape
    n_pages = page_table.shape[1]
    n_heads = width // HEAD_DIM
    moba = cache_f is None
    one = pl.BlockSpec((None, 1, width), lambda s, pt: (s, 0, 0))

    def paged(arr, p):
        return pl.BlockSpec((None, None) + arr.shape[2:], lambda s, pt: (pt[s, p], layer_idx, 0, 0))

    in_specs = [one, one, one]
    args = [q, kn, vn]
    if not moba:
        in_specs.append(pl.BlockSpec((None, n_heads, 1), lambda s, pt: (s, 0, 0)))
        args.append(lf_new)
    in_specs += [paged(cache_k, p) for p in range(n_pages)] + [paged(cache_v, p) for p in range(n_pages)]
    args += [cache_k] * n_pages + [cache_v] * n_pages
    if not moba:
        in_specs += [paged(cache_f, p) for p in range(n_pages)]
        args += [cache_f] * n_pages
    return pl.pallas_call(
        functools.partial(_paged_body, moba=moba, n_pages=n_pages, n_heads=n_heads),
        grid_spec=pltpu.PrefetchScalarGridSpec(
            num_scalar_prefetch=1,
            grid=(n,),
            in_specs=in_specs,
            out_specs=pl.BlockSpec((None, 1, width), lambda s, pt: (s, 0, 0)),
        ),
        out_shape=jax.ShapeDtypeStruct((n, 1, width), F32),
        compiler_params=_params("arbitrary"),
        name="moba_paged" if moba else "fox_paged",
    )(page_table, *args)


def _head_ones(width):
    i = np.arange(width) // HEAD_DIM
    return jnp.asarray(i[:, None] == i[None, :], dtype=BF16)


def _decay_placement(n_heads):
    place = np.zeros((DECAY_TERMS * LANES, n_heads // HEADS_PER_TILE * LANES), np.float32)
    for c in range(DECAY_TERMS):
        for h in range(n_heads):
            col = (h // HEADS_PER_TILE) * LANES + DECAY_LANE0 + DECAY_TERMS * (h % HEADS_PER_TILE) + c
            place[c * LANES + h, col] = 1.0
    return jnp.asarray(place, dtype=BF16)


def kernel(x_prompt, x_sample, cache_a_k, cache_a_v, cache_d_k, cache_d_v, cache_d_logf, state_c_conv, page_table, p_prompt, p_sample, norm_ffn1, ffn1_wi, ffn1_wo, norm_mix, ev_w_in, a_q_norm, a_k_norm, b_v_norm, b_w_s, b_bias, ev_w_out, od_w_in, od_b_f, c_conv_w, d_q_norm, d_k_norm, od_w_out, norm_ffn2, ffn2_wi, ffn2_wo, norm_ple, ple_w_gate, ple_w_proj):
    bp, s_len, d = x_prompt.shape
    bs, t_len, _ = x_sample.shape
    depth = norm_ffn1.shape[0]
    n_pool, _, page, h_a, hd = cache_a_k.shape
    h_d = cache_d_k.shape[3]
    n_pages = page_table.shape[1]
    d_a, d_d = h_a * hd, h_d * hd
    d_c = c_conv_w.shape[-1]
    assert hd == HEAD_DIM and t_len == 1 and s_len % MOBA_BLOCK == 0 and d_a == d_d == d_c
    assert (n_pages * page) % MOBA_BLOCK == 0 and b_w_s.shape[-1] == GMLP_CHUNK
    tp, ts = bp * s_len, bs * t_len
    nb = s_len // MOBA_BLOCK

    bf = lambda w: w.astype(BF16)
    vec = lambda g: g.reshape(g.shape[0], 1, -1)
    ffn1_wi, ffn1_wo, ffn2_wi, ffn2_wo = bf(ffn1_wi), bf(ffn1_wo), bf(ffn2_wi), bf(ffn2_wo)
    ple_wg, ple_wp = bf(ple_w_gate), bf(ple_w_proj)
    ev_w, ev_wo, od_wo = bf(ev_w_in), bf(ev_w_out), bf(od_w_out)
    n_main = 3 * d_c + 3 * d_d
    od_w = bf(od_w_in[:, :, :n_main])
    od_wf = od_w_in[:, :, n_main:]
    od_wf = bf(jnp.pad(od_wf, ((0, 0), (0, 0), (0, LANES - h_d))))
    od_bf = jnp.pad(od_b_f, ((0, 0), (0, LANES - h_d)))[:, None, :]
    g_ffn1, g_mix, g_ffn2, g_ple = vec(norm_ffn1), vec(norm_mix), vec(norm_ffn2), vec(norm_ple)
    a_qg = jnp.tile(a_q_norm, (1, h_a))[:, None, :]
    a_kg = jnp.tile(a_k_norm, (1, h_a))[:, None, :]
    b_vn = vec(b_v_norm)
    d_qg = jnp.tile(d_q_norm, (1, h_d))[:, None, :]
    d_kg = jnp.tile(d_k_norm, (1, h_d))[:, None, :]
    conv_w = jnp.pad(c_conv_w, ((0, 0), (0, SUBLANES - CONV_W), (0, 0)))
    gate_bias = jnp.repeat(jnp.swapaxes(b_bias, 1, 2), HEAD_DIM, axis=2)
    gate_w1 = jnp.repeat(b_w_s[:, :, 0, 0], HEAD_DIM, axis=1)[:, None, :]
    gate_b1 = jnp.repeat(b_bias[:, :, 0], HEAD_DIM, axis=1)[:, None, :]
    e_ones = _head_ones(d_a)
    place = _decay_placement(h_d)

    pos_minor = lambda c: jnp.transpose(c, (0, 1, 3, 4, 2)).reshape(n_pool, c.shape[1], -1, page)
    ck_a, cv_a, ck_d, cv_d = pos_minor(cache_a_k), pos_minor(cache_a_v), pos_minor(cache_d_k), pos_minor(cache_d_v)
    cf_d = jnp.swapaxes(cache_d_logf, 2, 3)
    pp = p_prompt.reshape(depth, tp, -1)
    ps = p_sample.reshape(depth, ts, -1)

    xp = x_prompt.reshape(tp, d)
    xs = x_sample.reshape(ts, d)
    n_even, n_odd = (depth + 1) // 2, depth // 2
    a_kv = d_kv = None
    ak_s, av_s, bv_s = [], [], []
    cv_p, cv_s = [], []
    df_p, dk_s, dv_s, df_s = [], [], [], []

    for l in range(depth):
        xp = _ffn_half(xp, g_ffn1, ffn1_wi, ffn1_wo, l)
        xs = _ffn_half(xs, g_ffn1, ffn1_wi, ffn1_wo, l)
        finish = functools.partial(_layer_tail, g2=g_ffn2, wi=ffn2_wi, wo=ffn2_wo, gp=g_ple, wg=ple_wg, wp=ple_wp,
                                 layer=l)
        if l % 2 == 0:
            e = l // 2
            q, kt, vt, u, vg, kaug, vaug, km = _even_in(
                xp, g_mix, ev_w, a_qg, a_kg, b_vn, e_ones, l, e, True, s_len, stacked=a_kv, n_slots=n_even)
            a_kv = (kt, vt)
            km = jnp.pad(km.reshape(bp, nb, d_a), ((0, 0), (0, LANES - nb), (0, 0))).reshape(bp * LANES, d_a)
            att = _flash(q, kaug, vaug, km, bp, s_len)
            sg = _gmlp_gate(u, vg, b_w_s, gate_bias, e)
            xp = finish(xp, (att, sg), ev_wo, e, p=pp)

            q, k, v, u, vg = _even_in(xs, g_mix, ev_w, a_qg, a_kg, b_vn, e_ones, l, e, False, s_len)
            r3 = lambda a: a.reshape(bs, 1, -1)
            att = _paged_attn(page_table, r3(q), r3(k), r3(v), ck_a, cv_a, e).reshape(ts, d_a)
            xs = finish(xs, (att, u, vg), ev_wo, e, p=ps, gate_vecs=(gate_w1[e], gate_b1[e]))
            ak_s.append(k.reshape(bs, t_len, h_a, hd))
            av_s.append(v.reshape(bs, t_len, h_a, hd))
            bv_s.append(vg.reshape(bs, t_len, -1))
        else:
            o = l // 2
            q, kt, vt, yc, lf, kaug, vaug, tail = _odd_in_prompt(
                xp, g_mix, od_w, od_wf, od_bf, d_qg, d_kg, conv_w, e_ones, place, l, o, s_len, h_d,
                stacked=d_kv, n_slots=n_odd)
            d_kv = (kt, vt)
            att = _flash(q, kaug, vaug, None, bp, s_len)
            xp = finish(xp, (yc, att), od_wo, o, p=pp)
            cv_p.append(tail[:, SUBLANES - (CONV_W - 1):, :])
            df_p.append(lf.reshape(bp, s_len, h_d))

            buf = state_c_conv[:, o]
            yc, q, k, v, lf, pre = _odd_in_sample(
                xs, g_mix, od_w, od_wf, od_bf, d_qg, d_kg, conv_w, e_ones,
                buf[:, 0], buf[:, 1], l, o)
            r3 = lambda a: a.reshape(bs, 1, -1)
            lf = lf[:, :h_d]
            att = _paged_attn(page_table, r3(q), r3(k), r3(v), ck_d, cv_d, o,
                              lf_new=lf.reshape(bs, h_d, 1), cache_f=cf_d).reshape(ts, d_d)
            xs = finish(xs, (yc, att), od_wo, o, p=ps)
            cv_s.append(jnp.stack([buf[:, 1], pre], axis=1))
            dk_s.append(k.reshape(bs, t_len, h_d, hd))
            dv_s.append(v.reshape(bs, t_len, h_d, hd))
            df_s.append(lf.reshape(bs, t_len, h_d))

    st = lambda rows: jnp.stack(rows, axis=1)
    seq_major = lambda a, heads: jnp.transpose(a.reshape(bp, a.shape[1], heads, hd, s_len), (0, 1, 4, 2, 3))
    return (xp.reshape(bp, s_len, d), xs.reshape(bs, t_len, d),
            seq_major(a_kv[0], h_a), seq_major(a_kv[1], h_a), st(ak_s), st(av_s), st(bv_s),
            st(cv_p), st(cv_s),
            seq_major(d_kv[0], h_d), seq_major(d_kv[1], h_d), st(df_p), st(dk_s), st(dv_s), st(df_s))
```

```python
import functools
import math

import numpy as np
import jax
import jax.numpy as jnp
from jax import lax
from jax.experimental import pallas as pl
from jax.experimental.pallas import tpu as pltpu

F32 = jnp.float32
BF16 = jnp.bfloat16

HEAD_DIM = 64
MOBA_BLOCK = 256
MOBA_TOPK = 3
GMLP_CHUNK = 128
CONV_W = 3
RMS_EPS = 1e-6
NEG_INF = -1e30
ATTN_SCALE = 1.0 / math.sqrt(HEAD_DIM)
LOG2E = math.log2(math.e)

LANES = 128
SUBLANES = 8
MXU_WIDTH = 256
HEADS_PER_TILE = LANES // HEAD_DIM
DECAY_LANE0 = 96
DECAY_TERMS = 3
PAIR_ROWS = LANES + 16
VMEM_LIMIT = 56 * 1024 * 1024
TOKEN_TILES = (512, 256, 128, 64, 32, 16, 8)
HIGHEST = lax.Precision.HIGHEST


def _params(*sem):
    return pltpu.CompilerParams(dimension_semantics=sem, vmem_limit_bytes=VMEM_LIMIT)


def _pick(n, options):
    for t in options:
        if n % t == 0:
            return t
    raise ValueError(f"no tile in {options} divides {n}")


def _rms(x, g):
    return x * lax.rsqrt(jnp.mean(x * x, axis=-1, keepdims=True) + RMS_EPS) * g


def _dot(a, b):
    return jnp.dot(a, b, preferred_element_type=F32)


def _dot_t(a, b, precision=None):
    return lax.dot_general(a, b, (((1,), (1,)), ((), ())), precision=precision,
                           preferred_element_type=F32)


def _split3(x):
    hi = x.astype(BF16)
    r = x - hi.astype(F32)
    mid = r.astype(BF16)
    lo = (r - mid.astype(F32)).astype(BF16)
    return hi, mid, lo


def _tree(op, xs):
    xs = list(xs)
    while len(xs) > 1:
        xs = [op(xs[i], xs[i + 1]) if i + 1 < len(xs) else xs[i] for i in range(0, len(xs), 2)]
    return xs[0]


def _log_sigmoid(x):
    return jnp.minimum(x, 0.0) - jnp.log1p(jnp.exp(-jnp.abs(x)))


def _head_norm(t, e, gain):
    ss = _dot((t * t).astype(BF16), e)
    return t * lax.rsqrt(ss * (1.0 / HEAD_DIM) + RMS_EPS) * gain


def _swiglu(x, g, wi_ref, wo_ref, chunks):
    f = wo_ref.shape[0]
    hn = _rms(x, g).astype(BF16)

    def gate_up(c0, c1):
        return _dot(hn, wi_ref[:, c0:c1]), _dot(hn, wi_ref[:, f + c0:f + c1])

    nxt = gate_up(*chunks[0])
    acc = None
    for i, (c0, c1) in enumerate(chunks):
        a, b = nxt
        if i + 1 < len(chunks):
            nxt = gate_up(*chunks[i + 1])
        part = _dot((jax.nn.silu(a) * b).astype(BF16), wo_ref[c0:c1, :])
        acc = part if acc is None else acc + part
    return acc


def _ffn_chunks(f):
    step = 2 * MXU_WIDTH
    chunks = tuple((c0, min(c0 + step, f)) for c0 in range(0, f, step))
    assert all((c1 - c0) % MXU_WIDTH == 0 for c0, c1 in chunks)
    return chunks


RESIDENT = dict(pipeline_mode=pl.Buffered(1))


def _ffn_body(x_ref, g_ref, wi_ref, wo_ref, o_ref, *, chunks):
    x = x_ref[...]
    o_ref[...] = x + 0.5 * _swiglu(x, g_ref[...], wi_ref, wo_ref, chunks)


def _ffn_half(x, g, wi, wo, layer):
    t, d = x.shape
    f = wo.shape[1]
    tm = _pick(t, TOKEN_TILES)
    return pl.pallas_call(
        functools.partial(_ffn_body, chunks=_ffn_chunks(f)),
        grid=(t // tm,),
        in_specs=[
            pl.BlockSpec((tm, d), lambda i: (i, 0)),
            pl.BlockSpec((None, 1, d), lambda i: (layer, 0, 0)),
            pl.BlockSpec((None, d, 2 * f), lambda i: (layer, 0, 0), **RESIDENT),
            pl.BlockSpec((None, f, d), lambda i: (layer, 0, 0), **RESIDENT),
        ],
        out_specs=pl.BlockSpec((tm, d), lambda i: (i, 0)),
        out_shape=jax.ShapeDtypeStruct((t, d), F32),
        compiler_params=_params("parallel"),
        name="ffn_half",
    )(x, g, wi, wo)


def _tail_body(*refs, chunks, gated):
    if gated:
        x_ref, a_ref, u_ref, vg_ref, wv_ref, bv_ref = refs[:6]
        b = u_ref[...] * (wv_ref[...] * vg_ref[...] + bv_ref[...])
        rest = refs[6:]
    else:
        x_ref, a_ref, b_ref = refs[:3]
        b = b_ref[...]
        rest = refs[3:]
    wm_ref, g2_ref, wi_ref, wo_ref, p_ref, gp_ref, wg_ref, wp_ref, o_ref = rest
    half = a_ref.shape[-1]
    proj = _dot(p_ref[...].astype(BF16), wp_ref[...])
    x = (x_ref[...] + _dot(a_ref[...].astype(BF16), wm_ref[0:half, :])
         + _dot(b.astype(BF16), wm_ref[half:2 * half, :]))
    x = x + 0.5 * _swiglu(x, g2_ref[...], wi_ref, wo_ref, chunks)
    gate = jax.nn.sigmoid(_dot(_rms(x, gp_ref[...]).astype(BF16), wg_ref[...]))
    o_ref[...] = x + gate * proj


def _layer_tail(x, parts, wm, mixer, g2, wi, wo, p, gp, wg, wp, layer, gate_vecs=None):
    t, d = x.shape
    half = parts[0].shape[-1]
    f = wo.shape[1]
    pdim = p.shape[-1]
    tm = _pick(t, TOKEN_TILES)
    row = lambda width: pl.BlockSpec((tm, width), lambda i: (i, 0))
    vec = pl.BlockSpec((1, half), lambda i: (0, 0))
    norm = pl.BlockSpec((None, 1, d), lambda i: (layer, 0, 0))
    in_specs = [row(d)] + [row(half)] * len(parts)
    args = [x, *parts]
    if gate_vecs is not None:
        in_specs += [vec, vec]
        args += list(gate_vecs)
    in_specs += [pl.BlockSpec((None, 2 * half, d), lambda i: (mixer, 0, 0), **RESIDENT),
                 norm,
                 pl.BlockSpec((None, d, 2 * f), lambda i: (layer, 0, 0), **RESIDENT),
                 pl.BlockSpec((None, f, d), lambda i: (layer, 0, 0), **RESIDENT),
                 pl.BlockSpec((None, tm, pdim), lambda i: (layer, i, 0)),
                 norm,
                 pl.BlockSpec((None, d, d), lambda i: (layer, 0, 0), **RESIDENT),
                 pl.BlockSpec((None, pdim, d), lambda i: (layer, 0, 0), **RESIDENT)]
    args += [wm, g2, wi, wo, p, gp, wg, wp]
    return pl.pallas_call(
        functools.partial(_tail_body, chunks=_ffn_chunks(f), gated=gate_vecs is not None),
        grid=(t // tm,),
        in_specs=in_specs,
        out_specs=row(d),
        out_shape=jax.ShapeDtypeStruct((t, d), F32),
        compiler_params=_params("parallel"),
        name="layer_tail",
    )(*args)


def _gmlp_body(u_ref, vg_ref, ws_ref, bias_ref, o_ref):
    tm = u_ref.shape[0]
    c = GMLP_CHUNK
    r_i = lax.broadcasted_iota(jnp.int32, (c, c), 0)
    c_i = lax.broadcasted_iota(jnp.int32, (c, c), 1)
    tril = c_i <= r_i
    first_head = c_i < HEAD_DIM
    n_groups = ws_ref.shape[0]
    w = [jnp.where(tril, ws_ref[g], 0.0).astype(BF16) for g in range(n_groups)]
    bias = bias_ref[...]
    for ci in range(tm // c):
        rows = pl.ds(ci * c, c)
        vg = vg_ref[rows, :].astype(BF16)
        tiles = []
        for t in range(n_groups // HEADS_PER_TILE):
            vt = vg[:, t * LANES:(t + 1) * LANES]
            tiles.append(jnp.where(first_head, _dot(w[2 * t], vt), _dot(w[2 * t + 1], vt)))
        mixed = jnp.concatenate(tiles, axis=1) + bias
        o_ref[rows, :] = u_ref[rows, :] * mixed


def _gmlp_gate(u, vg, ws, bias, layer):
    t, width = u.shape
    tm = _pick(t, TOKEN_TILES)
    g, c = ws.shape[1], ws.shape[2]
    row = pl.BlockSpec((tm, width), lambda i: (i, 0))
    return pl.pallas_call(
        _gmlp_body,
        grid=(t // tm,),
        in_specs=[row, row,
                  pl.BlockSpec((None, g, c, c), lambda i: (layer, 0, 0, 0)),
                  pl.BlockSpec((None, c, width), lambda i: (layer, 0, 0))],
        out_specs=row,
        out_shape=jax.ShapeDtypeStruct((t, width), F32),
        compiler_params=_params("parallel"),
        name="gmlp_gate",
    )(u, vg, ws, bias)


def _block_onehot(tile_rows, first_pos, nb):
    r = lax.broadcasted_iota(jnp.int32, (tile_rows, LANES), 0)
    c = lax.broadcasted_iota(jnp.int32, (tile_rows, LANES), 1)
    blk = lax.div(first_pos + r, MOBA_BLOCK)
    return jnp.where((c == blk) & (c < nb), 1.0, 0.0)


def _store_value_tiles(vaug_ref, v_t):
    ones = jnp.ones((PAIR_ROWS - LANES, MOBA_BLOCK), BF16)
    for j in range(vaug_ref.shape[0]):
        for t in range(v_t.shape[0] // LANES):
            tile = v_t[t * LANES:(t + 1) * LANES, j * MOBA_BLOCK:(j + 1) * MOBA_BLOCK]
            vaug_ref[j, t * PAIR_ROWS:t * PAIR_ROWS + LANES, :] = tile.astype(BF16)
            vaug_ref[j, t * PAIR_ROWS + LANES:(t + 1) * PAIR_ROWS, :] = ones


def _even_in_body(x_ref, g_ref, w_ref, qg_ref, kg_ref, vn_ref, e_ref, *rest, prompt, seq_len, n_alias, chunks):
    x = x_ref[...]
    if prompt:
        g1_ref, wi_ref, wo_ref = rest[:3]
        rest = rest[3:]
        x = x + 0.5 * _swiglu(x, g1_ref[...], wi_ref, wo_ref, chunks)
        rest[-1][...] = x
    outs = rest[n_alias:]
    q_ref, k_ref, v_ref, u_ref, vg_ref = outs[:5]
    half = q_ref.shape[-1]
    h = _rms(x, g_ref[...]).astype(BF16)
    e = e_ref[...]
    part = lambda n: _dot(h, w_ref[:, n * half:(n + 1) * half])
    z_q, z_k = part(0), part(1)
    q_ref[...] = _head_norm(z_q, e, qg_ref[...])
    z_vg = part(4)
    k = _head_norm(z_k, e, kg_ref[...])
    z_u = part(3)
    vg_ref[...] = _head_norm(jax.nn.gelu(z_vg), e, vn_ref[...])
    v = part(2)
    u_ref[...] = jax.nn.gelu(z_u)
    if prompt:
        v_t = v.T
        k_ref[...] = k.T
        v_ref[...] = v_t
    else:
        k_ref[...] = k
        v_ref[...] = v
    if prompt:
        kaug_ref, vaug_ref, km_ref = outs[5:8]
        tm = x_ref.shape[0]
        first_pos = lax.rem(pl.program_id(0) * tm, seq_len)
        onehot = _block_onehot(tm, first_pos, seq_len // MOBA_BLOCK).astype(BF16)
        kb = k.astype(BF16)
        for t in range(half // LANES):
            kaug_ref[:, 2 * t * LANES:(2 * t + 1) * LANES] = kb[:, t * LANES:(t + 1) * LANES]
            kaug_ref[:, (2 * t + 1) * LANES:(2 * t + 2) * LANES] = onehot
        _store_value_tiles(vaug_ref, v_t)
        for b in range(tm // MOBA_BLOCK):
            km_ref[b:b + 1, :] = jnp.mean(k[b * MOBA_BLOCK:(b + 1) * MOBA_BLOCK, :], axis=0, keepdims=True)


def _value_tiles_out(t, tm, half):
    per_step = tm // MOBA_BLOCK
    rows = half // LANES * PAIR_ROWS
    spec = pl.BlockSpec((per_step, rows, MOBA_BLOCK), lambda i: (i, 0, 0))
    return spec, jax.ShapeDtypeStruct((t // MOBA_BLOCK, rows, MOBA_BLOCK), BF16)


def _stacked_kv(stacked, n_slots, batch, width, seq_len, slot, tm, first_alias_input):
    tiles_per_seq = seq_len // tm
    spec = pl.BlockSpec((None, None, width, tm), lambda i: (i // tiles_per_seq, slot, 0, i % tiles_per_seq))
    shape = jax.ShapeDtypeStruct((batch, n_slots, width, seq_len), F32)
    if stacked is None:
        return spec, shape, [], [], {}
    any_spec = pl.BlockSpec(memory_space=pl.ANY)
    return spec, shape, [any_spec, any_spec], list(stacked), {first_alias_input: 1, first_alias_input + 1: 2}


def _ffn_inputs(ffn, layer, d):
    g1, wi, wo = ffn
    f = wo.shape[1]
    specs = [pl.BlockSpec((None, 1, d), lambda i: (layer, 0, 0)),
             pl.BlockSpec((None, d, 2 * f), lambda i: (layer, 0, 0), **RESIDENT),
             pl.BlockSpec((None, f, d), lambda i: (layer, 0, 0), **RESIDENT)]
    return specs, [g1, wi, wo], _ffn_chunks(f)


def _even_in(x, g, w, qg, kg, vn, e, layer, ev, prompt, seq_len, stacked=None, n_slots=1, ffn=None):
    t, d = x.shape
    n_in = w.shape[-1]
    half = n_in // 5
    tm = _pick(t, (512, 256) if prompt else TOKEN_TILES)
    row = lambda width: pl.BlockSpec((tm, width), lambda i: (i, 0))
    vec = lambda arr: pl.BlockSpec((None, 1, arr.shape[-1]), lambda i: (ev, 0, 0))
    in_specs = [row(d),
                pl.BlockSpec((None, 1, d), lambda i: (layer, 0, 0)),
                pl.BlockSpec((None, d, n_in), lambda i: (ev, 0, 0), **RESIDENT),
                vec(qg), vec(kg), vec(vn),
                pl.BlockSpec(e.shape, lambda i: (0, 0), **RESIDENT)]
    args = [x, g, w, qg, kg, vn, e]
    out_specs = [row(half)] * 5
    out_shape = [jax.ShapeDtypeStruct((t, half), F32)] * 5
    aliases = {}
    chunks = None
    if prompt:
        ffn_specs, ffn_args, chunks = _ffn_inputs(ffn, layer, d)
        in_specs += ffn_specs
        args += ffn_args
        kv_spec, kv_shape, alias_specs, alias_args, aliases = _stacked_kv(
            stacked, n_slots, t // seq_len, half, seq_len, ev, tm, len(in_specs))
        in_specs += alias_specs
        args += alias_args
        nkm = tm // MOBA_BLOCK
        vaug_spec, vaug_shape = _value_tiles_out(t, tm, half)
        out_specs = [row(half), kv_spec, kv_spec, row(half), row(half),
                     row(2 * half), vaug_spec, pl.BlockSpec((None, nkm, half), lambda i: (i, 0, 0)), row(d)]
        out_shape = [out_shape[0], kv_shape, kv_shape, out_shape[0], out_shape[0],
                     jax.ShapeDtypeStruct((t, 2 * half), BF16), vaug_shape,
                     jax.ShapeDtypeStruct((t // tm, nkm, half), F32), jax.ShapeDtypeStruct((t, d), F32)]
    return pl.pallas_call(
        functools.partial(_even_in_body, prompt=prompt, seq_len=seq_len, n_alias=len(aliases), chunks=chunks),
        grid=(t // tm,),
        in_specs=in_specs,
        out_specs=out_specs,
        out_shape=out_shape,
        input_output_aliases=aliases,
        compiler_params=_params("parallel"),
        name="even_in",
    )(*args)


def _odd_in_prompt_body(x_ref, g_ref, w_ref, wf_ref, bf_ref, qg_ref, kg_ref, cw_ref, e_ref, p_ref,
                        g1_ref, wi_ref, wo_ref, *rest, seq_len, n_alias, chunks):
    (q_ref, k_ref, v_ref, yc_ref, lf_ref, kaug_ref, vaug_ref, tail_ref, x1_ref,
     pre_ref, carry_ref) = rest[n_alias:]
    tm = x_ref.shape[0]
    half = q_ref.shape[-1]
    n_heads = lf_ref.shape[-1]
    tiles_per_seq = seq_len // tm
    s_idx = lax.rem(pl.program_id(0), tiles_per_seq)

    @pl.when(s_idx == 0)
    def _():
        pre_ref[0:SUBLANES, :] = jnp.zeros((SUBLANES, half), F32)
        carry_ref[...] = jnp.zeros_like(carry_ref)

    @pl.when(s_idx != 0)
    def _():
        pre_ref[0:SUBLANES, :] = pre_ref[tm:tm + SUBLANES, :]

    x = x_ref[...]
    x = x + 0.5 * _swiglu(x, g1_ref[...], wi_ref, wo_ref, chunks)
    x1_ref[...] = x
    h = _rms(x, g_ref[...]).astype(BF16)
    e = e_ref[...]
    part = lambda n: _dot(h, w_ref[:, n * half:(n + 1) * half])
    f_logit = _dot(h, wf_ref[...])
    z_gc, z_hc = part(1), part(2)

    lane = lax.broadcasted_iota(jnp.int32, (tm, LANES), 1)
    logf = jnp.where(lane < n_heads, _log_sigmoid(f_logit + bf_ref[...]), 0.0)
    lf_ref[...] = logf[:, 0:n_heads]
    r_i = lax.broadcasted_iota(jnp.int32, (tm, tm), 0)
    c_i = lax.broadcasted_iota(jnp.int32, (tm, tm), 1)
    tril = jnp.where(c_i <= r_i, 1.0, 0.0).astype(BF16)
    hi, mid, lo = _split3(logf)
    cum = _dot(tril, hi) + _dot(tril, mid) + _dot(tril, lo) + carry_ref[...]
    carry_ref[...] = cum[tm - 1:tm, :]
    gb = part(0)

    pre = z_gc * z_hc
    pre_ref[SUBLANES:SUBLANES + tm, :] = pre
    conv = (cw_ref[0:1, :] * pre_ref[SUBLANES - 2:SUBLANES - 2 + tm, :]
            + cw_ref[1:2, :] * pre_ref[SUBLANES - 1:SUBLANES - 1 + tm, :]
            + cw_ref[2:3, :] * pre)
    z_q = part(3)
    yc_ref[...] = gb * conv
    tail_ref[...] = pre[tm - SUBLANES:tm, :]
    z_k = part(4)
    q_ref[...] = _head_norm(z_q, e, qg_ref[...])
    v = part(5)
    k = _head_norm(z_k, e, kg_ref[...])
    v_t = v.T
    k_ref[...] = k.T
    v_ref[...] = v_t
    _store_value_tiles(vaug_ref, v_t)

    dec = _dot(jnp.concatenate(_split3(cum * (-LOG2E)), axis=1), p_ref[...])
    onehot = _block_onehot(tm, s_idx * tm, seq_len // MOBA_BLOCK)
    kb = k.astype(BF16)
    for t in range(half // LANES):
        kaug_ref[:, 2 * t * LANES:(2 * t + 1) * LANES] = kb[:, t * LANES:(t + 1) * LANES]
        kaug_ref[:, (2 * t + 1) * LANES:(2 * t + 2) * LANES] = (
            dec[:, t * LANES:(t + 1) * LANES] + onehot).astype(BF16)


def _odd_in_prompt(x, g, w, wf, bf, qg, kg, cw, e, place, layer, od, seq_len, n_heads, ffn, stacked=None, n_slots=1):
    t, d = x.shape
    half = qg.shape[-1]
    tm = _pick(seq_len, (512, 256))
    row = lambda width: pl.BlockSpec((tm, width), lambda i: (i, 0))
    vec = lambda arr, **kw: pl.BlockSpec((None,) + arr.shape[1:], lambda i: (od, 0, 0), **kw)
    full = lambda arr: pl.BlockSpec(arr.shape, lambda i: (0, 0), **RESIDENT)
    tiles_per_seq = seq_len // tm
    ffn_specs, ffn_args, chunks = _ffn_inputs(ffn, layer, d)
    in_specs = [row(d),
                pl.BlockSpec((None, 1, d), lambda i: (layer, 0, 0)),
                vec(w, **RESIDENT), vec(wf, **RESIDENT), vec(bf), vec(qg), vec(kg), vec(cw), full(e), full(place)]
    in_specs += ffn_specs
    kv_spec, kv_shape, alias_specs, alias_args, aliases = _stacked_kv(
        stacked, n_slots, t // seq_len, half, seq_len, od, tm, len(in_specs))
    vaug_spec, vaug_shape = _value_tiles_out(t, tm, half)
    return pl.pallas_call(
        functools.partial(_odd_in_prompt_body, seq_len=seq_len, n_alias=len(aliases), chunks=chunks),
        grid=(t // tm,),
        in_specs=in_specs + alias_specs,
        out_specs=[row(half), kv_spec, kv_spec, row(half), row(n_heads), row(2 * half), vaug_spec,
                   pl.BlockSpec((None, SUBLANES, half), lambda i: (i // tiles_per_seq, 0, 0)), row(d)],
        out_shape=[jax.ShapeDtypeStruct((t, half), F32), kv_shape, kv_shape, jax.ShapeDtypeStruct((t, half), F32),
                   jax.ShapeDtypeStruct((t, n_heads), F32),
                   jax.ShapeDtypeStruct((t, 2 * half), BF16), vaug_shape,
                   jax.ShapeDtypeStruct((t // seq_len, SUBLANES, half), F32), jax.ShapeDtypeStruct((t, d), F32)],
        scratch_shapes=[pltpu.VMEM((tm + 2 * SUBLANES, half), F32), pltpu.VMEM((1, LANES), F32)],
        input_output_aliases=aliases,
        compiler_params=_params("arbitrary"),
        name="odd_in_prompt",
    )(x, g, w, wf, bf, qg, kg, cw, e, place, *ffn_args, *alias_args)


def _odd_in_sample_body(x_ref, g_ref, w_ref, wf_ref, bf_ref, qg_ref, kg_ref, cw_ref, e_ref,
                        b0_ref, b1_ref, yc_ref, q_ref, k_ref, v_ref, lf_ref, pre_ref):
    half = q_ref.shape[-1]
    h = _rms(x_ref[...], g_ref[...]).astype(BF16)
    z = _dot(h, w_ref[...])
    e = e_ref[...]
    pre = z[:, half:2 * half] * z[:, 2 * half:3 * half]
    conv = cw_ref[0:1, :] * b0_ref[...] + cw_ref[1:2, :] * b1_ref[...] + cw_ref[2:3, :] * pre
    yc_ref[...] = z[:, 0:half] * conv
    pre_ref[...] = pre
    q_ref[...] = _head_norm(z[:, 3 * half:4 * half], e, qg_ref[...])
    k_ref[...] = _head_norm(z[:, 4 * half:5 * half], e, kg_ref[...])
    v_ref[...] = z[:, 5 * half:6 * half]
    lf_ref[...] = _log_sigmoid(_dot(h, wf_ref[...]) + bf_ref[...])


def _odd_in_sample(x, g, w, wf, bf, qg, kg, cw, e, b0, b1, layer, od):
    t, d = x.shape
    half = qg.shape[-1]
    tm = _pick(t, TOKEN_TILES)
    row = lambda width: pl.BlockSpec((tm, width), lambda i: (i, 0))
    vec = lambda arr: pl.BlockSpec((None,) + arr.shape[1:], lambda i: (od, 0, 0))
    return pl.pallas_call(
        _odd_in_sample_body,
        grid=(t // tm,),
        in_specs=[row(d),
                  pl.BlockSpec((None, 1, d), lambda i: (layer, 0, 0)),
                  vec(w), vec(wf), vec(bf), vec(qg), vec(kg), vec(cw),
                  pl.BlockSpec(e.shape, lambda i: (0, 0)), row(half), row(half)],
        out_specs=[row(half)] * 4 + [row(LANES), row(half)],
        out_shape=[jax.ShapeDtypeStruct((t, half), F32)] * 4
        + [jax.ShapeDtypeStruct((t, LANES), F32), jax.ShapeDtypeStruct((t, half), F32)],
        compiler_params=_params("parallel"),
        name="odd_in_sample",
    )(x, g, w, wf, bf, qg, kg, cw, e, b0, b1)


def _flash_body(q_ref, kaug_ref, v_ref, *rest, moba, nb, tk, n_tiles):
    if moba:
        km_ref, o_ref, qa_ref = rest
    else:
        o_ref, qa_ref = rest
    tq = MOBA_BLOCK
    qi = pl.program_id(2)
    lane = lax.broadcasted_iota(jnp.int32, (tq, LANES), 1)
    is_blk = lane < nb
    past = lane < qi
    low = lane < HEAD_DIM
    nb_rows = -(-nb // SUBLANES) * SUBLANES
    blk_t = lax.broadcasted_iota(jnp.int32, (nb_rows, tq), 0)
    past_t = blk_t < qi
    blk_f = blk_t.astype(F32)
    own_rows = pl.ds(pl.multiple_of(qi * tq, tq), tq)
    r_i = lax.broadcasted_iota(jnp.int32, (tq, tq), 0)
    c_i = lax.broadcasted_iota(jnp.int32, (tq, tq), 1)
    causal = r_i <= c_i
    causal = jnp.concatenate([causal] * HEADS_PER_TILE, axis=1)
    chains = [(t, hh) for t in range(n_tiles) for hh in range(HEADS_PER_TILE)]

    q_heads = [jnp.where(low if hh == 0 else ~low, q_ref[:, t * LANES:(t + 1) * LANES], 0.0) for t, hh in chains]
    if moba:
        gates = [_dot_t(km_ref[0:nb_rows, t * LANES:(t + 1) * LANES], q_heads[c], precision=HIGHEST)
                 for c, (t, hh) in enumerate(chains)]
    qa_own = []
    for c, (t, hh) in enumerate(chains):
        qh = q_heads[c]
        if moba:
            gate = jnp.where(past_t, gates[c], NEG_INF)
            bias_t = jnp.full((nb_rows, tq), NEG_INF, F32)
            for _ in range(MOBA_TOPK):
                top = jnp.max(gate, axis=0, keepdims=True)
                idx = jnp.min(jnp.where(gate == top, blk_f, float(LANES)), axis=0, keepdims=True)
                pick = blk_f == idx
                bias_t = jnp.where(pick, jnp.where(past_t, 0.0, NEG_INF), bias_t)
                gate = jnp.where(pick, -jnp.inf, gate)
            bias = jnp.concatenate([bias_t, jnp.full((LANES - nb_rows, tq), NEG_INF, F32)], axis=0).T
            aux_own = jnp.zeros((tq, LANES), F32)
        else:
            bias = jnp.where(past, 0.0, NEG_INF)
            d0 = DECAY_LANE0 + DECAY_TERMS * hh
            aux_own = jnp.where((lane >= d0) & (lane < d0 + DECAY_TERMS), 1.0, 0.0)
        aux = jnp.where(is_blk, bias, aux_own)
        qb = (qh * (ATTN_SCALE * LOG2E)).astype(BF16)
        qa_ref[t, hh * tq:(hh + 1) * tq, :] = jnp.concatenate([qb, aux.astype(BF16)], axis=1)
        qa_own.append(jnp.concatenate([qb, aux_own.astype(BF16)], axis=1))

    own_scores = [_dot_t(kaug_ref[own_rows, 2 * t * LANES:(2 * t + 2) * LANES],
                         jnp.concatenate(qa_own[HEADS_PER_TILE * t:HEADS_PER_TILE * (t + 1)], axis=0))
                  for t in range(n_tiles)]
    init = []
    for t in range(n_tiles):
        s = jnp.where(causal, own_scores[t], NEG_INF)
        m = jnp.max(s, axis=0, keepdims=True)
        init.append((m, _dot(v_ref[qi, t * PAIR_ROWS:(t + 1) * PAIR_ROWS, :], jnp.exp2(s - m).astype(BF16))))

    blocks_per_step = tk // tq

    def body(g, carry):
        rows = pl.ds(pl.multiple_of(g * tk, tk), tk)
        scores = [_dot_t(kaug_ref[rows, 2 * t * LANES:(2 * t + 2) * LANES], qa_ref[t])
                  for t in range(n_tiles)]
        new = []
        for t in range(n_tiles):
            m, acc = carry[t]
            m_new = jnp.maximum(m, jnp.max(scores[t], axis=0, keepdims=True))
            p = jnp.exp2(scores[t] - m_new).astype(BF16)
            pv = _dot(v_ref[g * blocks_per_step, t * PAIR_ROWS:(t + 1) * PAIR_ROWS, :], p[0:tq, :])
            for j in range(1, blocks_per_step):
                pv = pv + _dot(v_ref[g * blocks_per_step + j, t * PAIR_ROWS:(t + 1) * PAIR_ROWS, :],
                               p[j * tq:(j + 1) * tq, :])
            new.append((m_new, jnp.exp2(m - m_new) * acc + pv))
        return tuple(new)

    n_steps = lax.div(qi + (blocks_per_step - 1), blocks_per_step)
    final = lax.fori_loop(0, n_steps, body, tuple(init))
    for t in range(n_tiles):
        _, acc = final[t]
        inv = 1.0 / acc[LANES:LANES + 1, :]
        out_t = jnp.concatenate([acc[0:HEAD_DIM, 0:tq] * inv[:, 0:tq],
                                 acc[HEAD_DIM:LANES, tq:2 * tq] * inv[:, tq:2 * tq]], axis=0)
        o_ref[:, t * LANES:(t + 1) * LANES] = out_t.T


def _flash(q, kaug, vaug, kmean, batch, seq_len):
    t, width = q.shape
    tq = MOBA_BLOCK
    nq = seq_len // tq
    nb = seq_len // MOBA_BLOCK
    assert nb <= DECAY_LANE0 and DECAY_LANE0 + HEADS_PER_TILE * DECAY_TERMS <= LANES
    tk = _pick(seq_len, (1024, 512, 256))
    n_tiles = width // LANES
    cols = n_tiles * LANES
    moba = kmean is not None
    in_specs = [pl.BlockSpec((tq, cols), lambda b, hp, i: (b * nq + i, hp)),
                pl.BlockSpec((seq_len, 2 * cols), lambda b, hp, i: (b, hp), pipeline_mode=pl.Buffered(1)),
                pl.BlockSpec((nq, n_tiles * PAIR_ROWS, tq), lambda b, hp, i: (b, hp, 0),
                             pipeline_mode=pl.Buffered(1))]
    args = [q, kaug, vaug]
    if moba:
        in_specs.append(pl.BlockSpec((LANES, cols), lambda b, hp, i: (b, hp)))
        args.append(kmean)
    return pl.pallas_call(
        functools.partial(_flash_body, moba=moba, nb=nb, tk=tk, n_tiles=n_tiles),
        grid=(batch, width // cols, nq),
        in_specs=in_specs,
        out_specs=pl.BlockSpec((tq, cols), lambda b, hp, i: (b * nq + i, hp)),
        out_shape=jax.ShapeDtypeStruct((t, width), F32),
        scratch_shapes=[pltpu.VMEM((n_tiles, HEADS_PER_TILE * tq, 2 * LANES), BF16)],
        compiler_params=_params("parallel", "parallel", "arbitrary"),
        name="moba_attn" if moba else "fox_attn",
    )(*args)


def _paged_body(pt_ref, q_ref, kn_ref, vn_ref, *rest, moba, n_pages, n_heads):
    del pt_ref
    if moba:
        k_refs, v_refs, (o_ref,) = rest[:n_pages], rest[n_pages:2 * n_pages], rest[2 * n_pages:]
    else:
        lfn_ref = rest[0]
        k_refs, v_refs = rest[1:1 + n_pages], rest[1 + n_pages:1 + 2 * n_pages]
        f_refs, (o_ref,) = rest[1 + 2 * n_pages:1 + 3 * n_pages], rest[1 + 3 * n_pages:]
    page = k_refs[0].shape[1]
    width = q_ref.shape[-1]
    assert n_heads == SUBLANES
    q = q_ref[...]
    h_i = lax.broadcasted_iota(jnp.int32, (n_heads, width), 0)
    w_i = lax.broadcasted_iota(jnp.int32, (n_heads, width), 1)
    own = lax.div(w_i, HEAD_DIM) == h_i
    q8 = jnp.where(own, q, 0.0)
    h2_i = lax.broadcasted_iota(jnp.int32, (2 * n_heads, width), 0)
    w2_i = lax.broadcasted_iota(jnp.int32, (2 * n_heads, width), 1)
    qs = jnp.where(lax.div(w2_i, HEAD_DIM) == lax.rem(h2_i, n_heads), q, 0.0) * ATTN_SCALE
    qs_hi = qs.astype(BF16).astype(F32)
    q16 = jnp.where(h2_i < n_heads, qs_hi, qs - qs_hi).astype(BF16)
    s = []
    for p in range(n_pages):
        s16 = _dot(q16, k_refs[p][...].astype(BF16))
        s.append(s16[0:n_heads, :] + s16[n_heads:2 * n_heads, :])

    if moba:
        pages_per_blk = MOBA_BLOCK // page
        n_blk = n_pages // pages_per_blk
        assert MOBA_TOPK <= n_blk
        gate = [jnp.sum(_tree(jnp.add, s[b * pages_per_blk:(b + 1) * pages_per_blk]), axis=-1, keepdims=True)
                for b in range(n_blk)]
        for b in range(n_blk):
            beaten_by = [jnp.where(gate[o] >= gate[b] if o < b else gate[o] > gate[b], 1.0, 0.0)
                         for o in range(n_blk) if o != b]
            on = _tree(jnp.add, beaten_by) < float(MOBA_TOPK)
            for p in range(b * pages_per_blk, (b + 1) * pages_per_blk):
                s[p] = jnp.where(on, s[p], NEG_INF)
    else:
        r_i = lax.broadcasted_iota(jnp.int32, (page, page), 0)
        c_i = lax.broadcasted_iota(jnp.int32, (page, page), 1)
        after_in_page = jnp.where(r_i > c_i, 1.0, 0.0).astype(BF16)
        f_pages = [f_refs[p][...] for p in range(n_pages)]
        within = []
        for f in f_pages:
            r = _dot(jnp.concatenate(_split3(f), axis=0), after_in_page)
            within.append(r[0:n_heads] + r[n_heads:2 * n_heads] + r[2 * n_heads:3 * n_heads])
        totals = [jnp.sum(f, axis=-1, keepdims=True) for f in f_pages]
        after = lfn_ref[...]
        for p in reversed(range(n_pages)):
            s[p] = s[p] + (within[p] + after)
            after = after + totals[p]

    s_self = jnp.sum(q8 * kn_ref[...], axis=-1, keepdims=True) * ATTN_SCALE
    m = jnp.maximum(s_self, jnp.max(_tree(jnp.maximum, s), axis=-1, keepdims=True))
    p_self = jnp.exp(s_self - m)
    e = [jnp.exp(s[p] - m) for p in range(n_pages)]
    l = p_self + jnp.sum(_tree(jnp.add, e), axis=-1, keepdims=True)
    acc = None
    for p in range(n_pages):
        term = v_refs[p][...].reshape(n_heads, HEAD_DIM, page) * e[p][:, None, :]
        acc = term if acc is None else acc + term
    past = _dot_t(jnp.ones((SUBLANES, page), BF16), acc.reshape(width, page).astype(BF16))[0:1, :]
    per_lane = lambda col: jnp.sum(jnp.where(own, col, 0.0), axis=0, keepdims=True)
    o_ref[...] = (past + per_lane(p_self) * vn_ref[...]) / per_lane(l)


def _paged_attn(page_table, q, kn, vn, cache_k, cache_v, layer_idx, lf_new=None, cache_f=None):
    n, _, width = q.shape
    n_pages = page_table.shape[1]
    n_heads = width // HEAD_DIM
    moba = cache_f is None
    one = pl.BlockSpec((None, 1, width), lambda s, pt: (s, 0, 0))

    def paged(arr, p):
        return pl.BlockSpec((None, None) + arr.shape[2:], lambda s, pt: (pt[s, p], layer_idx, 0, 0))

    in_specs = [one, one, one]
    args = [q, kn, vn]
    if not moba:
        in_specs.append(pl.BlockSpec((None, n_heads, 1), lambda s, pt: (s, 0, 0)))
        args.append(lf_new)
    in_specs += [paged(cache_k, p) for p in range(n_pages)] + [paged(cache_v, p) for p in range(n_pages)]
    args += [cache_k] * n_pages + [cache_v] * n_pages
    if not moba:
        in_specs += [paged(cache_f, p) for p in range(n_pages)]
        args += [cache_f] * n_pages
    return pl.pallas_call(
        functools.partial(_paged_body, moba=moba, n_pages=n_pages, n_heads=n_heads),
        grid_spec=pltpu.PrefetchScalarGridSpec(
            num_scalar_prefetch=1,
            grid=(n,),
            in_specs=in_specs,
            out_specs=pl.BlockSpec((None, 1, width), lambda s, pt: (s, 0, 0)),
        ),
        out_shape=jax.ShapeDtypeStruct((n, 1, width), F32),
        compiler_params=_params("arbitrary"),
        name="moba_paged" if moba else "fox_paged",
    )(page_table, *args)


def _head_ones(width):
    i = np.arange(width) // HEAD_DIM
    return jnp.asarray(i[:, None] == i[None, :], dtype=BF16)


def _decay_placement(n_heads):
    place = np.zeros((DECAY_TERMS * LANES, n_heads // HEADS_PER_TILE * LANES), np.float32)
    for c in range(DECAY_TERMS):
        for h in range(n_heads):
            col = (h // HEADS_PER_TILE) * LANES + DECAY_LANE0 + DECAY_TERMS * (h % HEADS_PER_TILE) + c
            place[c * LANES + h, col] = 1.0
    return jnp.asarray(place, dtype=BF16)


def kernel(x_prompt, x_sample, cache_a_k, cache_a_v, cache_d_k, cache_d_v, cache_d_logf, state_c_conv, page_table, p_prompt, p_sample, norm_ffn1, ffn1_wi, ffn1_wo, norm_mix, ev_w_in, a_q_norm, a_k_norm, b_v_norm, b_w_s, b_bias, ev_w_out, od_w_in, od_b_f, c_conv_w, d_q_norm, d_k_norm, od_w_out, norm_ffn2, ffn2_wi, ffn2_wo, norm_ple, ple_w_gate, ple_w_proj):
    bp, s_len, d = x_prompt.shape
    bs, t_len, _ = x_sample.shape
    depth = norm_ffn1.shape[0]
    n_pool, _, page, h_a, hd = cache_a_k.shape
    h_d = cache_d_k.shape[3]
    n_pages = page_table.shape[1]
    d_a, d_d = h_a * hd, h_d * hd
    d_c = c_conv_w.shape[-1]
    assert hd == HEAD_DIM and t_len == 1 and s_len % MOBA_BLOCK == 0 and d_a == d_d == d_c
    assert (n_pages * page) % MOBA_BLOCK == 0 and b_w_s.shape[-1] == GMLP_CHUNK
    tp, ts = bp * s_len, bs * t_len
    nb = s_len // MOBA_BLOCK

    bf = lambda w: w.astype(BF16)
    vec = lambda g: g.reshape(g.shape[0], 1, -1)
    ffn1_wi, ffn1_wo, ffn2_wi, ffn2_wo = bf(ffn1_wi), bf(ffn1_wo), bf(ffn2_wi), bf(ffn2_wo)
    ple_wg, ple_wp = bf(ple_w_gate), bf(ple_w_proj)
    ev_w, ev_wo, od_wo = bf(ev_w_in), bf(ev_w_out), bf(od_w_out)
    n_main = 3 * d_c + 3 * d_d
    od_w = bf(od_w_in[:, :, :n_main])
    od_wf = od_w_in[:, :, n_main:]
    od_wf = bf(jnp.pad(od_wf, ((0, 0), (0, 0), (0, LANES - h_d))))
    od_bf = jnp.pad(od_b_f, ((0, 0), (0, LANES - h_d)))[:, None, :]
    g_ffn1, g_mix, g_ffn2, g_ple = vec(norm_ffn1), vec(norm_mix), vec(norm_ffn2), vec(norm_ple)
    a_qg = jnp.tile(a_q_norm, (1, h_a))[:, None, :]
    a_kg = jnp.tile(a_k_norm, (1, h_a))[:, None, :]
    b_vn = vec(b_v_norm)
    d_qg = jnp.tile(d_q_norm, (1, h_d))[:, None, :]
    d_kg = jnp.tile(d_k_norm, (1, h_d))[:, None, :]
    conv_w = jnp.pad(c_conv_w, ((0, 0), (0, SUBLANES - CONV_W), (0, 0)))
    gate_bias = jnp.repeat(jnp.swapaxes(b_bias, 1, 2), HEAD_DIM, axis=2)
    gate_w1 = jnp.repeat(b_w_s[:, :, 0, 0], HEAD_DIM, axis=1)[:, None, :]
    gate_b1 = jnp.repeat(b_bias[:, :, 0], HEAD_DIM, axis=1)[:, None, :]
    e_ones = _head_ones(d_a)
    place = _decay_placement(h_d)

    pos_minor = lambda c: jnp.transpose(c, (0, 1, 3, 4, 2)).reshape(n_pool, c.shape[1], -1, page)
    ck_a, cv_a, ck_d, cv_d = pos_minor(cache_a_k), pos_minor(cache_a_v), pos_minor(cache_d_k), pos_minor(cache_d_v)
    cf_d = jnp.swapaxes(cache_d_logf, 2, 3)
    pp = p_prompt.reshape(depth, tp, -1)
    ps = p_sample.reshape(depth, ts, -1)

    xp = x_prompt.reshape(tp, d)
    xs = x_sample.reshape(ts, d)
    n_even, n_odd = (depth + 1) // 2, depth // 2
    a_kv = d_kv = None
    ak_s, av_s, bv_s = [], [], []
    cv_p, cv_s = [], []
    df_p, dk_s, dv_s, df_s = [], [], [], []

    for l in range(depth):
        ffn1 = (g_ffn1, ffn1_wi, ffn1_wo)
        xs = _ffn_half(xs, *ffn1, l)
        finish = functools.partial(_layer_tail, g2=g_ffn2, wi=ffn2_wi, wo=ffn2_wo, gp=g_ple, wg=ple_wg, wp=ple_wp,
                                 layer=l)
        if l % 2 == 0:
            e = l // 2
            q, kt, vt, u, vg, kaug, vaug, km, xp = _even_in(
                xp, g_mix, ev_w, a_qg, a_kg, b_vn, e_ones, l, e, True, s_len, stacked=a_kv, n_slots=n_even,
                ffn=ffn1)
            a_kv = (kt, vt)
            km = jnp.pad(km.reshape(bp, nb, d_a), ((0, 0), (0, LANES - nb), (0, 0))).reshape(bp * LANES, d_a)
            att = _flash(q, kaug, vaug, km, bp, s_len)
            sg = _gmlp_gate(u, vg, b_w_s, gate_bias, e)
            xp = finish(xp, (att, sg), ev_wo, e, p=pp)

            q, k, v, u, vg = _even_in(xs, g_mix, ev_w, a_qg, a_kg, b_vn, e_ones, l, e, False, s_len)
            r3 = lambda a: a.reshape(bs, 1, -1)
            att = _paged_attn(page_table, r3(q), r3(k), r3(v), ck_a, cv_a, e).reshape(ts, d_a)
            xs = finish(xs, (att, u, vg), ev_wo, e, p=ps, gate_vecs=(gate_w1[e], gate_b1[e]))
            ak_s.append(k.reshape(bs, t_len, h_a, hd))
            av_s.append(v.reshape(bs, t_len, h_a, hd))
            bv_s.append(vg.reshape(bs, t_len, -1))
        else:
            o = l // 2
            q, kt, vt, yc, lf, kaug, vaug, tail, xp = _odd_in_prompt(
                xp, g_mix, od_w, od_wf, od_bf, d_qg, d_kg, conv_w, e_ones, place, l, o, s_len, h_d, ffn1,
                stacked=d_kv, n_slots=n_odd)
            d_kv = (kt, vt)
            att = _flash(q, kaug, vaug, None, bp, s_len)
            xp = finish(xp, (yc, att), od_wo, o, p=pp)
            cv_p.append(tail[:, SUBLANES - (CONV_W - 1):, :])
            df_p.append(lf.reshape(bp, s_len, h_d))

            buf = state_c_conv[:, o]
            yc, q, k, v, lf, pre = _odd_in_sample(
                xs, g_mix, od_w, od_wf, od_bf, d_qg, d_kg, conv_w, e_ones,
                buf[:, 0], buf[:, 1], l, o)
            r3 = lambda a: a.reshape(bs, 1, -1)
            lf = lf[:, :h_d]
            att = _paged_attn(page_table, r3(q), r3(k), r3(v), ck_d, cv_d, o,
                              lf_new=lf.reshape(bs, h_d, 1), cache_f=cf_d).reshape(ts, d_d)
            xs = finish(xs, (yc, att), od_wo, o, p=ps)
            cv_s.append(jnp.stack([buf[:, 1], pre], axis=1))
            dk_s.append(k.reshape(bs, t_len, h_d, hd))
            dv_s.append(v.reshape(bs, t_len, h_d, hd))
            df_s.append(lf.reshape(bs, t_len, h_d))

    st = lambda rows: jnp.stack(rows, axis=1)
    seq_major = lambda a, heads: jnp.transpose(a.reshape(bp, a.shape[1], heads, hd, s_len), (0, 1, 4, 2, 3))
    return (xp.reshape(bp, s_len, d), xs.reshape(bs, t_len, d),
            seq_major(a_kv[0], h_a), seq_major(a_kv[1], h_a), st(ak_s), st(av_s), st(bv_s),
            st(cv_p), st(cv_s),
            seq_major(d_kv[0], h_d), seq_major(d_kv[1], h_d), st(df_p), st(dk_s), st(dv_s), st(df_s))
```

```python
import functools
import math

import numpy as np
import jax
import jax.numpy as jnp
from jax import lax
from jax.experimental import pallas as pl
from jax.experimental.pallas import tpu as pltpu

F32 = jnp.float32
BF16 = jnp.bfloat16

HEAD_DIM = 64
MOBA_BLOCK = 256
MOBA_TOPK = 3
GMLP_CHUNK = 128
CONV_W = 3
RMS_EPS = 1e-6
NEG_INF = -1e30
ATTN_SCALE = 1.0 / math.sqrt(HEAD_DIM)
LOG2E = math.log2(math.e)

LANES = 128
SUBLANES = 8
MXU_WIDTH = 256
HEADS_PER_TILE = LANES // HEAD_DIM
DECAY_LANE0 = 96
DECAY_TERMS = 3
PAIR_ROWS = LANES + 16
VMEM_LIMIT = 56 * 1024 * 1024
TOKEN_TILES = (512, 256, 128, 64, 32, 16, 8)
HIGHEST = lax.Precision.HIGHEST


def _params(*sem):
    return pltpu.CompilerParams(dimension_semantics=sem, vmem_limit_bytes=VMEM_LIMIT)


def _pick(n, options):
    for t in options:
        if n % t == 0:
            return t
    raise ValueError(f"no tile in {options} divides {n}")


def _rms(x, g):
    return x * lax.rsqrt(jnp.mean(x * x, axis=-1, keepdims=True) + RMS_EPS) * g


def _dot(a, b):
    return jnp.dot(a, b, preferred_element_type=F32)


def _dot_t(a, b, precision=None):
    return lax.dot_general(a, b, (((1,), (1,)), ((), ())), precision=precision,
                           preferred_element_type=F32)


def _split3(x):
    hi = x.astype(BF16)
    r = x - hi.astype(F32)
    mid = r.astype(BF16)
    lo = (r - mid.astype(F32)).astype(BF16)
    return hi, mid, lo


def _tree(op, xs):
    xs = list(xs)
    while len(xs) > 1:
        xs = [op(xs[i], xs[i + 1]) if i + 1 < len(xs) else xs[i] for i in range(0, len(xs), 2)]
    return xs[0]


def _log_sigmoid(x):
    return jnp.minimum(x, 0.0) - jnp.log1p(jnp.exp(-jnp.abs(x)))


def _head_norm(t, e, gain):
    ss = _dot((t * t).astype(BF16), e)
    return t * lax.rsqrt(ss * (1.0 / HEAD_DIM) + RMS_EPS) * gain


def _swiglu(x, g, wi_ref, wo_ref, chunks):
    f = wo_ref.shape[0]
    hn = _rms(x, g).astype(BF16)

    def gate_up(c0, c1):
        return _dot(hn, wi_ref[:, c0:c1]), _dot(hn, wi_ref[:, f + c0:f + c1])

    nxt = gate_up(*chunks[0])
    acc = None
    for i, (c0, c1) in enumerate(chunks):
        a, b = nxt
        if i + 1 < len(chunks):
            nxt = gate_up(*chunks[i + 1])
        part = _dot((jax.nn.silu(a) * b).astype(BF16), wo_ref[c0:c1, :])
        acc = part if acc is None else acc + part
    return acc


def _ffn_chunks(f):
    step = 2 * MXU_WIDTH
    chunks = tuple((c0, min(c0 + step, f)) for c0 in range(0, f, step))
    assert all((c1 - c0) % MXU_WIDTH == 0 for c0, c1 in chunks)
    return chunks


RESIDENT = dict(pipeline_mode=pl.Buffered(1))


def _ffn_body(x_ref, g_ref, wi_ref, wo_ref, o_ref, *, chunks):
    x = x_ref[...]
    o_ref[...] = x + 0.5 * _swiglu(x, g_ref[...], wi_ref, wo_ref, chunks)


def _ffn_half(x, g, wi, wo, layer):
    t, d = x.shape
    f = wo.shape[1]
    tm = _pick(t, TOKEN_TILES)
    return pl.pallas_call(
        functools.partial(_ffn_body, chunks=_ffn_chunks(f)),
        grid=(t // tm,),
        in_specs=[
            pl.BlockSpec((tm, d), lambda i: (i, 0)),
            pl.BlockSpec((None, 1, d), lambda i: (layer, 0, 0)),
            pl.BlockSpec((None, d, 2 * f), lambda i: (layer, 0, 0), **RESIDENT),
            pl.BlockSpec((None, f, d), lambda i: (layer, 0, 0), **RESIDENT),
        ],
        out_specs=pl.BlockSpec((tm, d), lambda i: (i, 0)),
        out_shape=jax.ShapeDtypeStruct((t, d), F32),
        compiler_params=_params("parallel"),
        name="ffn_half",
    )(x, g, wi, wo)


def _tail_body(*refs, chunks, gated):
    if gated:
        x_ref, a_ref, u_ref, vg_ref, wv_ref, bv_ref = refs[:6]
        b = u_ref[...] * (wv_ref[...] * vg_ref[...] + bv_ref[...])
        rest = refs[6:]
    else:
        x_ref, a_ref, b_ref = refs[:3]
        b = b_ref[...]
        rest = refs[3:]
    wm_ref, g2_ref, wi_ref, wo_ref, p_ref, gp_ref, wg_ref, wp_ref, o_ref = rest
    half = a_ref.shape[-1]
    proj = _dot(p_ref[...].astype(BF16), wp_ref[...])
    x = (x_ref[...] + _dot(a_ref[...].astype(BF16), wm_ref[0:half, :])
         + _dot(b.astype(BF16), wm_ref[half:2 * half, :]))
    x = x + 0.5 * _swiglu(x, g2_ref[...], wi_ref, wo_ref, chunks)
    gate = jax.nn.sigmoid(_dot(_rms(x, gp_ref[...]).astype(BF16), wg_ref[...]))
    o_ref[...] = x + gate * proj


def _layer_tail(x, parts, wm, mixer, g2, wi, wo, p, gp, wg, wp, layer, gate_vecs=None):
    t, d = x.shape
    half = parts[0].shape[-1]
    f = wo.shape[1]
    pdim = p.shape[-1]
    tm = _pick(t, TOKEN_TILES)
    row = lambda width: pl.BlockSpec((tm, width), lambda i: (i, 0))
    vec = pl.BlockSpec((1, half), lambda i: (0, 0))
    norm = pl.BlockSpec((None, 1, d), lambda i: (layer, 0, 0))
    in_specs = [row(d)] + [row(half)] * len(parts)
    args = [x, *parts]
    if gate_vecs is not None:
        in_specs += [vec, vec]
        args += list(gate_vecs)
    in_specs += [pl.BlockSpec((None, 2 * half, d), lambda i: (mixer, 0, 0), **RESIDENT),
                 norm,
                 pl.BlockSpec((None, d, 2 * f), lambda i: (layer, 0, 0), **RESIDENT),
                 pl.BlockSpec((None, f, d), lambda i: (layer, 0, 0), **RESIDENT),
                 pl.BlockSpec((None, tm, pdim), lambda i: (layer, i, 0)),
                 norm,
                 pl.BlockSpec((None, d, d), lambda i: (layer, 0, 0), **RESIDENT),
                 pl.BlockSpec((None, pdim, d), lambda i: (layer, 0, 0), **RESIDENT)]
    args += [wm, g2, wi, wo, p, gp, wg, wp]
    return pl.pallas_call(
        functools.partial(_tail_body, chunks=_ffn_chunks(f), gated=gate_vecs is not None),
        grid=(t // tm,),
        in_specs=in_specs,
        out_specs=row(d),
        out_shape=jax.ShapeDtypeStruct((t, d), F32),
        compiler_params=_params("parallel"),
        name="layer_tail",
    )(*args)


def _gmlp_body(u_ref, vg_ref, ws_ref, bias_ref, o_ref):
    tm = u_ref.shape[0]
    c = GMLP_CHUNK
    r_i = lax.broadcasted_iota(jnp.int32, (c, c), 0)
    c_i = lax.broadcasted_iota(jnp.int32, (c, c), 1)
    tril = c_i <= r_i
    first_head = c_i < HEAD_DIM
    n_groups = ws_ref.shape[0]
    w = [jnp.where(tril, ws_ref[g], 0.0).astype(BF16) for g in range(n_groups)]
    bias = bias_ref[...]
    for ci in range(tm // c):
        rows = pl.ds(ci * c, c)
        vg = vg_ref[rows, :].astype(BF16)
        tiles = []
        for t in range(n_groups // HEADS_PER_TILE):
            vt = vg[:, t * LANES:(t + 1) * LANES]
            tiles.append(jnp.where(first_head, _dot(w[2 * t], vt), _dot(w[2 * t + 1], vt)))
        mixed = jnp.concatenate(tiles, axis=1) + bias
        o_ref[rows, :] = u_ref[rows, :] * mixed


def _gmlp_gate(u, vg, ws, bias, layer):
    t, width = u.shape
    tm = _pick(t, TOKEN_TILES)
    g, c = ws.shape[1], ws.shape[2]
    row = pl.BlockSpec((tm, width), lambda i: (i, 0))
    return pl.pallas_call(
        _gmlp_body,
        grid=(t // tm,),
        in_specs=[row, row,
                  pl.BlockSpec((None, g, c, c), lambda i: (layer, 0, 0, 0)),
                  pl.BlockSpec((None, c, width), lambda i: (layer, 0, 0))],
        out_specs=row,
        out_shape=jax.ShapeDtypeStruct((t, width), F32),
        compiler_params=_params("parallel"),
        name="gmlp_gate",
    )(u, vg, ws, bias)


def _block_onehot(tile_rows, first_pos, nb):
    r = lax.broadcasted_iota(jnp.int32, (tile_rows, LANES), 0)
    c = lax.broadcasted_iota(jnp.int32, (tile_rows, LANES), 1)
    blk = lax.div(first_pos + r, MOBA_BLOCK)
    return jnp.where((c == blk) & (c < nb), 1.0, 0.0)


def _store_value_tiles(vaug_ref, v_t):
    ones = jnp.ones((PAIR_ROWS - LANES, MOBA_BLOCK), BF16)
    for j in range(vaug_ref.shape[0]):
        for t in range(v_t.shape[0] // LANES):
            tile = v_t[t * LANES:(t + 1) * LANES, j * MOBA_BLOCK:(j + 1) * MOBA_BLOCK]
            vaug_ref[j, t * PAIR_ROWS:t * PAIR_ROWS + LANES, :] = tile.astype(BF16)
            vaug_ref[j, t * PAIR_ROWS + LANES:(t + 1) * PAIR_ROWS, :] = ones


def _even_in_body(x_ref, g_ref, w_ref, qg_ref, kg_ref, vn_ref, e_ref, *rest, prompt, seq_len, n_alias, chunks):
    x = x_ref[...]
    if prompt:
        g1_ref, wi_ref, wo_ref = rest[:3]
        rest = rest[3:]
        x = x + 0.5 * _swiglu(x, g1_ref[...], wi_ref, wo_ref, chunks)
        rest[-1][...] = x
    outs = rest[n_alias:]
    q_ref, k_ref, v_ref, u_ref, vg_ref = outs[:5]
    half = q_ref.shape[-1]
    h = _rms(x, g_ref[...]).astype(BF16)
    e = e_ref[...]
    part = lambda n: _dot(h, w_ref[:, n * half:(n + 1) * half])
    z_q, z_k = part(0), part(1)
    q_ref[...] = _head_norm(z_q, e, qg_ref[...])
    z_vg = part(4)
    k = _head_norm(z_k, e, kg_ref[...])
    z_u = part(3)
    vg_ref[...] = _head_norm(jax.nn.gelu(z_vg), e, vn_ref[...])
    v = part(2)
    u_ref[...] = jax.nn.gelu(z_u)
    if prompt:
        v_t = v.T
        k_ref[...] = k.T
        v_ref[...] = v_t
    else:
        k_ref[...] = k
        v_ref[...] = v
    if prompt:
        kaug_ref, vaug_ref, km_ref = outs[5:8]
        tm = x_ref.shape[0]
        first_pos = lax.rem(pl.program_id(0) * tm, seq_len)
        onehot = _block_onehot(tm, first_pos, seq_len // MOBA_BLOCK).astype(BF16)
        kb = k.astype(BF16)
        for t in range(half // LANES):
            kaug_ref[:, 2 * t * LANES:(2 * t + 1) * LANES] = kb[:, t * LANES:(t + 1) * LANES]
            kaug_ref[:, (2 * t + 1) * LANES:(2 * t + 2) * LANES] = onehot
        _store_value_tiles(vaug_ref, v_t)
        for b in range(tm // MOBA_BLOCK):
            km_ref[b:b + 1, :] = jnp.mean(k[b * MOBA_BLOCK:(b + 1) * MOBA_BLOCK, :], axis=0, keepdims=True)


def _value_tiles_out(t, tm, half):
    per_step = tm // MOBA_BLOCK
    rows = half // LANES * PAIR_ROWS
    spec = pl.BlockSpec((per_step, rows, MOBA_BLOCK), lambda i: (i, 0, 0))
    return spec, jax.ShapeDtypeStruct((t // MOBA_BLOCK, rows, MOBA_BLOCK), BF16)


def _stacked_kv(stacked, n_slots, batch, width, seq_len, slot, tm, first_alias_input):
    tiles_per_seq = seq_len // tm
    spec = pl.BlockSpec((None, None, width, tm), lambda i: (i // tiles_per_seq, slot, 0, i % tiles_per_seq))
    shape = jax.ShapeDtypeStruct((batch, n_slots, width, seq_len), F32)
    if stacked is None:
        return spec, shape, [], [], {}
    any_spec = pl.BlockSpec(memory_space=pl.ANY)
    return spec, shape, [any_spec, any_spec], list(stacked), {first_alias_input: 1, first_alias_input + 1: 2}


def _ffn_inputs(ffn, layer, d):
    g1, wi, wo = ffn
    f = wo.shape[1]
    specs = [pl.BlockSpec((None, 1, d), lambda i: (layer, 0, 0)),
             pl.BlockSpec((None, d, 2 * f), lambda i: (layer, 0, 0), **RESIDENT),
             pl.BlockSpec((None, f, d), lambda i: (layer, 0, 0), **RESIDENT)]
    return specs, [g1, wi, wo], _ffn_chunks(f)


def _even_in(x, g, w, qg, kg, vn, e, layer, ev, prompt, seq_len, stacked=None, n_slots=1, ffn=None):
    t, d = x.shape
    n_in = w.shape[-1]
    half = n_in // 5
    tm = _pick(t, (512, 256) if prompt else TOKEN_TILES)
    row = lambda width: pl.BlockSpec((tm, width), lambda i: (i, 0))
    vec = lambda arr: pl.BlockSpec((None, 1, arr.shape[-1]), lambda i: (ev, 0, 0))
    in_specs = [row(d),
                pl.BlockSpec((None, 1, d), lambda i: (layer, 0, 0)),
                pl.BlockSpec((None, d, n_in), lambda i: (ev, 0, 0), **RESIDENT),
                vec(qg), vec(kg), vec(vn),
                pl.BlockSpec(e.shape, lambda i: (0, 0), **RESIDENT)]
    args = [x, g, w, qg, kg, vn, e]
    out_specs = [row(half)] * 5
    out_shape = [jax.ShapeDtypeStruct((t, half), F32)] * 5
    aliases = {}
    chunks = None
    if prompt:
        ffn_specs, ffn_args, chunks = _ffn_inputs(ffn, layer, d)
        in_specs += ffn_specs
        args += ffn_args
        kv_spec, kv_shape, alias_specs, alias_args, aliases = _stacked_kv(
            stacked, n_slots, t // seq_len, half, seq_len, ev, tm, len(in_specs))
        in_specs += alias_specs
        args += alias_args
        nkm = tm // MOBA_BLOCK
        vaug_spec, vaug_shape = _value_tiles_out(t, tm, half)
        out_specs = [row(half), kv_spec, kv_spec, row(half), row(half),
                     row(2 * half), vaug_spec, pl.BlockSpec((None, nkm, half), lambda i: (i, 0, 0)), row(d)]
        out_shape = [out_shape[0], kv_shape, kv_shape, out_shape[0], out_shape[0],
                     jax.ShapeDtypeStruct((t, 2 * half), BF16), vaug_shape,
                     jax.ShapeDtypeStruct((t // tm, nkm, half), F32), jax.ShapeDtypeStruct((t, d), F32)]
    return pl.pallas_call(
        functools.partial(_even_in_body, prompt=prompt, seq_len=seq_len, n_alias=len(aliases), chunks=chunks),
        grid=(t // tm,),
        in_specs=in_specs,
        out_specs=out_specs,
        out_shape=out_shape,
        input_output_aliases=aliases,
        compiler_params=_params("parallel"),
        name="even_in",
    )(*args)


def _odd_in_prompt_body(x_ref, g_ref, w_ref, wf_ref, bf_ref, qg_ref, kg_ref, cw_ref, e_ref, p_ref,
                        g1_ref, wi_ref, wo_ref, *rest, seq_len, n_alias, chunks):
    (q_ref, k_ref, v_ref, yc_ref, lf_ref, kaug_ref, vaug_ref, tail_ref, x1_ref,
     pre_ref, carry_ref) = rest[n_alias:]
    tm = x_ref.shape[0]
    half = q_ref.shape[-1]
    n_heads = lf_ref.shape[-1]
    tiles_per_seq = seq_len // tm
    s_idx = lax.rem(pl.program_id(0), tiles_per_seq)

    @pl.when(s_idx == 0)
    def _():
        pre_ref[0:SUBLANES, :] = jnp.zeros((SUBLANES, half), F32)
        carry_ref[...] = jnp.zeros_like(carry_ref)

    @pl.when(s_idx != 0)
    def _():
        pre_ref[0:SUBLANES, :] = pre_ref[tm:tm + SUBLANES, :]

    x = x_ref[...]
    x = x + 0.5 * _swiglu(x, g1_ref[...], wi_ref, wo_ref, chunks)
    x1_ref[...] = x
    h = _rms(x, g_ref[...]).astype(BF16)
    e = e_ref[...]
    part = lambda n: _dot(h, w_ref[:, n * half:(n + 1) * half])
    f_logit = _dot(h, wf_ref[...])
    z_gc, z_hc = part(1), part(2)

    lane = lax.broadcasted_iota(jnp.int32, (tm, LANES), 1)
    logf = jnp.where(lane < n_heads, _log_sigmoid(f_logit + bf_ref[...]), 0.0)
    lf_ref[...] = logf[:, 0:n_heads]
    r_i = lax.broadcasted_iota(jnp.int32, (tm, tm), 0)
    c_i = lax.broadcasted_iota(jnp.int32, (tm, tm), 1)
    tril = jnp.where(c_i <= r_i, 1.0, 0.0).astype(BF16)
    hi, mid, lo = _split3(logf)
    cum = _dot(tril, hi) + _dot(tril, mid) + _dot(tril, lo) + carry_ref[...]
    carry_ref[...] = cum[tm - 1:tm, :]
    gb = part(0)

    pre = z_gc * z_hc
    pre_ref[SUBLANES:SUBLANES + tm, :] = pre
    conv = (cw_ref[0:1, :] * pre_ref[SUBLANES - 2:SUBLANES - 2 + tm, :]
            + cw_ref[1:2, :] * pre_ref[SUBLANES - 1:SUBLANES - 1 + tm, :]
            + cw_ref[2:3, :] * pre)
    z_q = part(3)
    yc_ref[...] = gb * conv
    tail_ref[...] = pre[tm - SUBLANES:tm, :]
    z_k = part(4)
    q_ref[...] = _head_norm(z_q, e, qg_ref[...])
    v = part(5)
    k = _head_norm(z_k, e, kg_ref[...])
    v_t = v.T
    k_ref[...] = k.T
    v_ref[...] = v_t
    _store_value_tiles(vaug_ref, v_t)

    dec = _dot(jnp.concatenate(_split3(cum * (-LOG2E)), axis=1), p_ref[...])
    onehot = _block_onehot(tm, s_idx * tm, seq_len // MOBA_BLOCK)
    kb = k.astype(BF16)
    for t in range(half // LANES):
        kaug_ref[:, 2 * t * LANES:(2 * t + 1) * LANES] = kb[:, t * LANES:(t + 1) * LANES]
        kaug_ref[:, (2 * t + 1) * LANES:(2 * t + 2) * LANES] = (
            dec[:, t * LANES:(t + 1) * LANES] + onehot).astype(BF16)


def _odd_in_prompt(x, g, w, wf, bf, qg, kg, cw, e, place, layer, od, seq_len, n_heads, ffn, stacked=None, n_slots=1):
    t, d = x.shape
    half = qg.shape[-1]
    tm = _pick(seq_len, (512, 256))
    row = lambda width: pl.BlockSpec((tm, width), lambda i: (i, 0))
    vec = lambda arr, **kw: pl.BlockSpec((None,) + arr.shape[1:], lambda i: (od, 0, 0), **kw)
    full = lambda arr: pl.BlockSpec(arr.shape, lambda i: (0, 0), **RESIDENT)
    tiles_per_seq = seq_len // tm
    ffn_specs, ffn_args, chunks = _ffn_inputs(ffn, layer, d)
    in_specs = [row(d),
                pl.BlockSpec((None, 1, d), lambda i: (layer, 0, 0)),
                vec(w, **RESIDENT), vec(wf, **RESIDENT), vec(bf), vec(qg), vec(kg), vec(cw), full(e), full(place)]
    in_specs += ffn_specs
    kv_spec, kv_shape, alias_specs, alias_args, aliases = _stacked_kv(
        stacked, n_slots, t // seq_len, half, seq_len, od, tm, len(in_specs))
    vaug_spec, vaug_shape = _value_tiles_out(t, tm, half)
    return pl.pallas_call(
        functools.partial(_odd_in_prompt_body, seq_len=seq_len, n_alias=len(aliases), chunks=chunks),
        grid=(t // tm,),
        in_specs=in_specs + alias_specs,
        out_specs=[row(half), kv_spec, kv_spec, row(half), row(n_heads), row(2 * half), vaug_spec,
                   pl.BlockSpec((None, SUBLANES, half), lambda i: (i // tiles_per_seq, 0, 0)), row(d)],
        out_shape=[jax.ShapeDtypeStruct((t, half), F32), kv_shape, kv_shape, jax.ShapeDtypeStruct((t, half), F32),
                   jax.ShapeDtypeStruct((t, n_heads), F32),
                   jax.ShapeDtypeStruct((t, 2 * half), BF16), vaug_shape,
                   jax.ShapeDtypeStruct((t // seq_len, SUBLANES, half), F32), jax.ShapeDtypeStruct((t, d), F32)],
        scratch_shapes=[pltpu.VMEM((tm + 2 * SUBLANES, half), F32), pltpu.VMEM((1, LANES), F32)],
        input_output_aliases=aliases,
        compiler_params=_params("arbitrary"),
        name="odd_in_prompt",
    )(x, g, w, wf, bf, qg, kg, cw, e, place, *ffn_args, *alias_args)


def _odd_in_sample_body(x_ref, g_ref, w_ref, wf_ref, bf_ref, qg_ref, kg_ref, cw_ref, e_ref,
                        b0_ref, b1_ref, yc_ref, q_ref, k_ref, v_ref, lf_ref, pre_ref):
    half = q_ref.shape[-1]
    h = _rms(x_ref[...], g_ref[...]).astype(BF16)
    z = _dot(h, w_ref[...])
    e = e_ref[...]
    pre = z[:, half:2 * half] * z[:, 2 * half:3 * half]
    conv = cw_ref[0:1, :] * b0_ref[...] + cw_ref[1:2, :] * b1_ref[...] + cw_ref[2:3, :] * pre
    yc_ref[...] = z[:, 0:half] * conv
    pre_ref[...] = pre
    q_ref[...] = _head_norm(z[:, 3 * half:4 * half], e, qg_ref[...])
    k_ref[...] = _head_norm(z[:, 4 * half:5 * half], e, kg_ref[...])
    v_ref[...] = z[:, 5 * half:6 * half]
    lf_ref[...] = _log_sigmoid(_dot(h, wf_ref[...]) + bf_ref[...])


def _odd_in_sample(x, g, w, wf, bf, qg, kg, cw, e, b0, b1, layer, od):
    t, d = x.shape
    half = qg.shape[-1]
    tm = _pick(t, TOKEN_TILES)
    row = lambda width: pl.BlockSpec((tm, width), lambda i: (i, 0))
    vec = lambda arr: pl.BlockSpec((None,) + arr.shape[1:], lambda i: (od, 0, 0))
    return pl.pallas_call(
        _odd_in_sample_body,
        grid=(t // tm,),
        in_specs=[row(d),
                  pl.BlockSpec((None, 1, d), lambda i: (layer, 0, 0)),
                  vec(w), vec(wf), vec(bf), vec(qg), vec(kg), vec(cw),
                  pl.BlockSpec(e.shape, lambda i: (0, 0)), row(half), row(half)],
        out_specs=[row(half)] * 4 + [row(LANES), row(half)],
        out_shape=[jax.ShapeDtypeStruct((t, half), F32)] * 4
        + [jax.ShapeDtypeStruct((t, LANES), F32), jax.ShapeDtypeStruct((t, half), F32)],
        compiler_params=_params("parallel"),
        name="odd_in_sample",
    )(x, g, w, wf, bf, qg, kg, cw, e, b0, b1)


def _flash_body(q_ref, kaug_ref, v_ref, *rest, moba, nb, tk, n_tiles):
    if moba:
        km_ref, o_ref, qa_ref = rest
    else:
        o_ref, qa_ref = rest
    tq = MOBA_BLOCK
    qi = pl.program_id(2)
    lane = lax.broadcasted_iota(jnp.int32, (tq, LANES), 1)
    is_blk = lane < nb
    past = lane < qi
    low = lane < HEAD_DIM
    nb_rows = -(-nb // SUBLANES) * SUBLANES
    blk_t = lax.broadcasted_iota(jnp.int32, (nb_rows, tq), 0)
    past_t = blk_t < qi
    blk_f = blk_t.astype(F32)
    own_rows = pl.ds(pl.multiple_of(qi * tq, tq), tq)
    r_i = lax.broadcasted_iota(jnp.int32, (tq, tq), 0)
    c_i = lax.broadcasted_iota(jnp.int32, (tq, tq), 1)
    causal = r_i <= c_i
    causal = jnp.concatenate([causal] * HEADS_PER_TILE, axis=1)
    chains = [(t, hh) for t in range(n_tiles) for hh in range(HEADS_PER_TILE)]

    q_heads = [jnp.where(low if hh == 0 else ~low, q_ref[:, t * LANES:(t + 1) * LANES], 0.0) for t, hh in chains]
    if moba:
        gates = [_dot_t(km_ref[0:nb_rows, t * LANES:(t + 1) * LANES], q_heads[c], precision=HIGHEST)
                 for c, (t, hh) in enumerate(chains)]
    qa_own = []
    for c, (t, hh) in enumerate(chains):
        qh = q_heads[c]
        if moba:
            gate = jnp.where(past_t, gates[c], NEG_INF)
            bias_t = jnp.full((nb_rows, tq), NEG_INF, F32)
            for _ in range(MOBA_TOPK):
                top = jnp.max(gate, axis=0, keepdims=True)
                idx = jnp.min(jnp.where(gate == top, blk_f, float(LANES)), axis=0, keepdims=True)
                pick = blk_f == idx
                bias_t = jnp.where(pick, jnp.where(past_t, 0.0, NEG_INF), bias_t)
                gate = jnp.where(pick, -jnp.inf, gate)
            bias = jnp.concatenate([bias_t, jnp.full((LANES - nb_rows, tq), NEG_INF, F32)], axis=0).T
            aux_own = jnp.zeros((tq, LANES), F32)
        else:
            bias = jnp.where(past, 0.0, NEG_INF)
            d0 = DECAY_LANE0 + DECAY_TERMS * hh
            aux_own = jnp.where((lane >= d0) & (lane < d0 + DECAY_TERMS), 1.0, 0.0)
        aux = jnp.where(is_blk, bias, aux_own)
        qb = (qh * (ATTN_SCALE * LOG2E)).astype(BF16)
        qa_ref[t, hh * tq:(hh + 1) * tq, :] = jnp.concatenate([qb, aux.astype(BF16)], axis=1)
        qa_own.append(jnp.concatenate([qb, aux_own.astype(BF16)], axis=1))

    own_scores = [_dot_t(kaug_ref[own_rows, 2 * t * LANES:(2 * t + 2) * LANES],
                         jnp.concatenate(qa_own[HEADS_PER_TILE * t:HEADS_PER_TILE * (t + 1)], axis=0))
                  for t in range(n_tiles)]
    init = []
    for t in range(n_tiles):
        s = jnp.where(causal, own_scores[t], NEG_INF)
        m = jnp.max(s, axis=0, keepdims=True)
        init.append((m, _dot(v_ref[qi, t * PAIR_ROWS:(t + 1) * PAIR_ROWS, :], jnp.exp2(s - m).astype(BF16))))

    blocks_per_step = tk // tq

    def step(n_blk, first_blk, carry):
        rows = pl.ds(pl.multiple_of(first_blk * tq, tq), n_blk * tq)
        scores = [_dot_t(kaug_ref[rows, 2 * t * LANES:(2 * t + 2) * LANES], qa_ref[t])
                  for t in range(n_tiles)]
        new = []
        for t in range(n_tiles):
            m, acc = carry[t]
            m_new = jnp.maximum(m, jnp.max(scores[t], axis=0, keepdims=True))
            p = jnp.exp2(scores[t] - m_new).astype(BF16)
            pv = _dot(v_ref[first_blk, t * PAIR_ROWS:(t + 1) * PAIR_ROWS, :], p[0:tq, :])
            for j in range(1, n_blk):
                pv = pv + _dot(v_ref[first_blk + j, t * PAIR_ROWS:(t + 1) * PAIR_ROWS, :], p[j * tq:(j + 1) * tq, :])
            new.append((m_new, jnp.exp2(m - m_new) * acc + pv))
        return tuple(new)

    n_full = lax.div(qi, blocks_per_step)
    carry = lax.fori_loop(0, n_full, lambda g, c: step(blocks_per_step, g * blocks_per_step, c), tuple(init))
    left = qi - n_full * blocks_per_step
    sizes = sorted({max(blocks_per_step // 2, 1), blocks_per_step})
    which = sum((left > s).astype(jnp.int32) for s in sizes[:-1]) + (left > 0).astype(jnp.int32)
    tails = [lambda c: c] + [functools.partial(step, s, n_full * blocks_per_step) for s in sizes]
    final = lax.switch(which, tails, carry) if blocks_per_step > 1 else carry
    for t in range(n_tiles):
        _, acc = final[t]
        inv = 1.0 / acc[LANES:LANES + 1, :]
        out_t = jnp.concatenate([acc[0:HEAD_DIM, 0:tq] * inv[:, 0:tq],
                                 acc[HEAD_DIM:LANES, tq:2 * tq] * inv[:, tq:2 * tq]], axis=0)
        o_ref[:, t * LANES:(t + 1) * LANES] = out_t.T


def _flash(q, kaug, vaug, kmean, batch, seq_len):
    t, width = q.shape
    tq = MOBA_BLOCK
    nq = seq_len // tq
    nb = seq_len // MOBA_BLOCK
    assert nb <= DECAY_LANE0 and DECAY_LANE0 + HEADS_PER_TILE * DECAY_TERMS <= LANES
    tk = _pick(seq_len, (1024, 512, 256))
    n_tiles = width // LANES
    cols = n_tiles * LANES
    moba = kmean is not None
    in_specs = [pl.BlockSpec((tq, cols), lambda b, hp, i: (b * nq + i, hp)),
                pl.BlockSpec((seq_len, 2 * cols), lambda b, hp, i: (b, hp), pipeline_mode=pl.Buffered(1)),
                pl.BlockSpec((nq, n_tiles * PAIR_ROWS, tq), lambda b, hp, i: (b, hp, 0),
                             pipeline_mode=pl.Buffered(1))]
    args = [q, kaug, vaug]
    if moba:
        in_specs.append(pl.BlockSpec((LANES, cols), lambda b, hp, i: (b, hp)))
        args.append(kmean)
    return pl.pallas_call(
        functools.partial(_flash_body, moba=moba, nb=nb, tk=tk, n_tiles=n_tiles),
        grid=(batch, width // cols, nq),
        in_specs=in_specs,
        out_specs=pl.BlockSpec((tq, cols), lambda b, hp, i: (b * nq + i, hp)),
        out_shape=jax.ShapeDtypeStruct((t, width), F32),
        scratch_shapes=[pltpu.VMEM((n_tiles, HEADS_PER_TILE * tq, 2 * LANES), BF16)],
        compiler_params=_params("parallel", "parallel", "arbitrary"),
        name="moba_attn" if moba else "fox_attn",
    )(*args)


def _paged_body(pt_ref, q_ref, kn_ref, vn_ref, *rest, moba, n_pages, n_heads):
    del pt_ref
    if moba:
        k_refs, v_refs, (o_ref,) = rest[:n_pages], rest[n_pages:2 * n_pages], rest[2 * n_pages:]
    else:
        lfn_ref = rest[0]
        k_refs, v_refs = rest[1:1 + n_pages], rest[1 + n_pages:1 + 2 * n_pages]
        f_refs, (o_ref,) = rest[1 + 2 * n_pages:1 + 3 * n_pages], rest[1 + 3 * n_pages:]
    page = k_refs[0].shape[1]
    width = q_ref.shape[-1]
    assert n_heads == SUBLANES
    q = q_ref[...]
    h_i = lax.broadcasted_iota(jnp.int32, (n_heads, width), 0)
    w_i = lax.broadcasted_iota(jnp.int32, (n_heads, width), 1)
    own = lax.div(w_i, HEAD_DIM) == h_i
    q8 = jnp.where(own, q, 0.0)
    h2_i = lax.broadcasted_iota(jnp.int32, (2 * n_heads, width), 0)
    w2_i = lax.broadcasted_iota(jnp.int32, (2 * n_heads, width), 1)
    qs = jnp.where(lax.div(w2_i, HEAD_DIM) == lax.rem(h2_i, n_heads), q, 0.0) * ATTN_SCALE
    qs_hi = qs.astype(BF16).astype(F32)
    q16 = jnp.where(h2_i < n_heads, qs_hi, qs - qs_hi).astype(BF16)
    s = []
    for p in range(n_pages):
        s16 = _dot(q16, k_refs[p][...].astype(BF16))
        s.append(s16[0:n_heads, :] + s16[n_heads:2 * n_heads, :])

    if moba:
        pages_per_blk = MOBA_BLOCK // page
        n_blk = n_pages // pages_per_blk
        assert MOBA_TOPK <= n_blk
        gate = [jnp.sum(_tree(jnp.add, s[b * pages_per_blk:(b + 1) * pages_per_blk]), axis=-1, keepdims=True)
                for b in range(n_blk)]
        for b in range(n_blk):
            beaten_by = [jnp.where(gate[o] >= gate[b] if o < b else gate[o] > gate[b], 1.0, 0.0)
                         for o in range(n_blk) if o != b]
            on = _tree(jnp.add, beaten_by) < float(MOBA_TOPK)
            for p in range(b * pages_per_blk, (b + 1) * pages_per_blk):
                s[p] = jnp.where(on, s[p], NEG_INF)
    else:
        r_i = lax.broadcasted_iota(jnp.int32, (page, page), 0)
        c_i = lax.broadcasted_iota(jnp.int32, (page, page), 1)
        after_in_page = jnp.where(r_i > c_i, 1.0, 0.0).astype(BF16)
        f_pages = [f_refs[p][...] for p in range(n_pages)]
        terms = []
        for f in f_pages:
            hi = f.astype(BF16).astype(F32)
            mid = (f - hi).astype(BF16).astype(F32)
            terms += [hi, mid, f - hi - mid]
        r = _dot(jnp.concatenate(terms, axis=0).astype(BF16), after_in_page)
        within = [r[(3 * p) * n_heads:(3 * p + 1) * n_heads] + r[(3 * p + 1) * n_heads:(3 * p + 2) * n_heads]
                  + r[(3 * p + 2) * n_heads:(3 * p + 3) * n_heads] for p in range(n_pages)]
        totals = [jnp.sum(f, axis=-1, keepdims=True) for f in f_pages]
        after = lfn_ref[...]
        for p in reversed(range(n_pages)):
            s[p] = s[p] + (within[p] + after)
            after = after + totals[p]

    s_self = jnp.sum(q8 * kn_ref[...], axis=-1, keepdims=True) * ATTN_SCALE
    m = jnp.maximum(s_self, jnp.max(_tree(jnp.maximum, s), axis=-1, keepdims=True))
    p_self = jnp.exp(s_self - m)
    e = [jnp.exp(s[p] - m) for p in range(n_pages)]
    l = p_self + jnp.sum(_tree(jnp.add, e), axis=-1, keepdims=True)
    acc = None
    for p in range(n_pages):
        term = v_refs[p][...].reshape(n_heads, HEAD_DIM, page) * e[p][:, None, :]
        acc = term if acc is None else acc + term
    past = _dot_t(jnp.ones((SUBLANES, page), BF16), acc.reshape(width, page).astype(BF16))[0:1, :]
    per_lane = lambda col: jnp.sum(jnp.where(own, col, 0.0), axis=0, keepdims=True)
    o_ref[...] = (past + per_lane(p_self) * vn_ref[...]) / per_lane(l)


def _paged_attn(page_table, q, kn, vn, cache_k, cache_v, layer_idx, lf_new=None, cache_f=None):
    n, _, width = q.shape
    n_pages = page_table.shape[1]
    n_heads = width // HEAD_DIM
    moba = cache_f is None
    one = pl.BlockSpec((None, 1, width), lambda s, pt: (s, 0, 0))

    def paged(arr, p):
        return pl.BlockSpec((None, None) + arr.shape[2:], lambda s, pt: (pt[s, p], layer_idx, 0, 0))

    in_specs = [one, one, one]
    args = [q, kn, vn]
    if not moba:
        in_specs.append(pl.BlockSpec((None, n_heads, 1), lambda s, pt: (s, 0, 0)))
        args.append(lf_new)
    in_specs += [paged(cache_k, p) for p in range(n_pages)] + [paged(cache_v, p) for p in range(n_pages)]
    args += [cache_k] * n_pages + [cache_v] * n_pages
    if not moba:
        in_specs += [paged(cache_f, p) for p in range(n_pages)]
        args += [cache_f] * n_pages
    return pl.pallas_call(
        functools.partial(_paged_body, moba=moba, n_pages=n_pages, n_heads=n_heads),
        grid_spec=pltpu.PrefetchScalarGridSpec(
            num_scalar_prefetch=1,
            grid=(n,),
            in_specs=in_specs,
            out_specs=pl.BlockSpec((None, 1, width), lambda s, pt: (s, 0, 0)),
        ),
        out_shape=jax.ShapeDtypeStruct((n, 1, width), F32),
        compiler_params=_params("arbitrary"),
        name="moba_paged" if moba else "fox_paged",
    )(page_table, *args)


def _head_ones(width):
    i = np.arange(width) // HEAD_DIM
    return jnp.asarray(i[:, None] == i[None, :], dtype=BF16)


def _decay_placement(n_heads):
    place = np.zeros((DECAY_TERMS * LANES, n_heads // HEADS_PER_TILE * LANES), np.float32)
    for c in range(DECAY_TERMS):
        for h in range(n_heads):
            col = (h // HEADS_PER_TILE) * LANES + DECAY_LANE0 + DECAY_TERMS * (h % HEADS_PER_TILE) + c
            place[c * LANES + h, col] = 1.0
    return jnp.asarray(place, dtype=BF16)


def kernel(x_prompt, x_sample, cache_a_k, cache_a_v, cache_d_k, cache_d_v, cache_d_logf, state_c_conv, page_table, p_prompt, p_sample, norm_ffn1, ffn1_wi, ffn1_wo, norm_mix, ev_w_in, a_q_norm, a_k_norm, b_v_norm, b_w_s, b_bias, ev_w_out, od_w_in, od_b_f, c_conv_w, d_q_norm, d_k_norm, od_w_out, norm_ffn2, ffn2_wi, ffn2_wo, norm_ple, ple_w_gate, ple_w_proj):
    bp, s_len, d = x_prompt.shape
    bs, t_len, _ = x_sample.shape
    depth = norm_ffn1.shape[0]
    n_pool, _, page, h_a, hd = cache_a_k.shape
    h_d = cache_d_k.shape[3]
    n_pages = page_table.shape[1]
    d_a, d_d = h_a * hd, h_d * hd
    d_c = c_conv_w.shape[-1]
    assert hd == HEAD_DIM and t_len == 1 and s_len % MOBA_BLOCK == 0 and d_a == d_d == d_c
    assert (n_pages * page) % MOBA_BLOCK == 0 and b_w_s.shape[-1] == GMLP_CHUNK
    tp, ts = bp * s_len, bs * t_len
    nb = s_len // MOBA_BLOCK

    bf = lambda w: w.astype(BF16)
    vec = lambda g: g.reshape(g.shape[0], 1, -1)
    ffn1_wi, ffn1_wo, ffn2_wi, ffn2_wo = bf(ffn1_wi), bf(ffn1_wo), bf(ffn2_wi), bf(ffn2_wo)
    ple_wg, ple_wp = bf(ple_w_gate), bf(ple_w_proj)
    ev_w, ev_wo, od_wo = bf(ev_w_in), bf(ev_w_out), bf(od_w_out)
    n_main = 3 * d_c + 3 * d_d
    od_w = bf(od_w_in[:, :, :n_main])
    od_wf = od_w_in[:, :, n_main:]
    od_wf = bf(jnp.pad(od_wf, ((0, 0), (0, 0), (0, LANES - h_d))))
    od_bf = jnp.pad(od_b_f, ((0, 0), (0, LANES - h_d)))[:, None, :]
    g_ffn1, g_mix, g_ffn2, g_ple = vec(norm_ffn1), vec(norm_mix), vec(norm_ffn2), vec(norm_ple)
    a_qg = jnp.tile(a_q_norm, (1, h_a))[:, None, :]
    a_kg = jnp.tile(a_k_norm, (1, h_a))[:, None, :]
    b_vn = vec(b_v_norm)
    d_qg = jnp.tile(d_q_norm, (1, h_d))[:, None, :]
    d_kg = jnp.tile(d_k_norm, (1, h_d))[:, None, :]
    conv_w = jnp.pad(c_conv_w, ((0, 0), (0, SUBLANES - CONV_W), (0, 0)))
    gate_bias = jnp.repeat(jnp.swapaxes(b_bias, 1, 2), HEAD_DIM, axis=2)
    gate_w1 = jnp.repeat(b_w_s[:, :, 0, 0], HEAD_DIM, axis=1)[:, None, :]
    gate_b1 = jnp.repeat(b_bias[:, :, 0], HEAD_DIM, axis=1)[:, None, :]
    e_ones = _head_ones(d_a)
    place = _decay_placement(h_d)

    pos_minor = lambda c: jnp.transpose(c, (0, 1, 3, 4, 2)).reshape(n_pool, c.shape[1], -1, page)
    ck_a, cv_a, ck_d, cv_d = pos_minor(cache_a_k), pos_minor(cache_a_v), pos_minor(cache_d_k), pos_minor(cache_d_v)
    cf_d = jnp.swapaxes(cache_d_logf, 2, 3)
    pp = p_prompt.reshape(depth, tp, -1)
    ps = p_sample.reshape(depth, ts, -1)

    xp = x_prompt.reshape(tp, d)
    xs = x_sample.reshape(ts, d)
    n_even, n_odd = (depth + 1) // 2, depth // 2
    a_kv = d_kv = None
    ak_s, av_s, bv_s = [], [], []
    cv_p, cv_s = [], []
    df_p, dk_s, dv_s, df_s = [], [], [], []

    for l in range(depth):
        ffn1 = (g_ffn1, ffn1_wi, ffn1_wo)
        xs = _ffn_half(xs, *ffn1, l)
        finish = functools.partial(_layer_tail, g2=g_ffn2, wi=ffn2_wi, wo=ffn2_wo, gp=g_ple, wg=ple_wg, wp=ple_wp,
                                 layer=l)
        if l % 2 == 0:
            e = l // 2
            q, kt, vt, u, vg, kaug, vaug, km, xp = _even_in(
                xp, g_mix, ev_w, a_qg, a_kg, b_vn, e_ones, l, e, True, s_len, stacked=a_kv, n_slots=n_even,
                ffn=ffn1)
            a_kv = (kt, vt)
            km = jnp.pad(km.reshape(bp, nb, d_a), ((0, 0), (0, LANES - nb), (0, 0))).reshape(bp * LANES, d_a)
            att = _flash(q, kaug, vaug, km, bp, s_len)
            sg = _gmlp_gate(u, vg, b_w_s, gate_bias, e)
            xp = finish(xp, (att, sg), ev_wo, e, p=pp)

            q, k, v, u, vg = _even_in(xs, g_mix, ev_w, a_qg, a_kg, b_vn, e_ones, l, e, False, s_len)
            r3 = lambda a: a.reshape(bs, 1, -1)
            att = _paged_attn(page_table, r3(q), r3(k), r3(v), ck_a, cv_a, e).reshape(ts, d_a)
            xs = finish(xs, (att, u, vg), ev_wo, e, p=ps, gate_vecs=(gate_w1[e], gate_b1[e]))
            ak_s.append(k.reshape(bs, t_len, h_a, hd))
            av_s.append(v.reshape(bs, t_len, h_a, hd))
            bv_s.append(vg.reshape(bs, t_len, -1))
        else:
            o = l // 2
            q, kt, vt, yc, lf, kaug, vaug, tail, xp = _odd_in_prompt(
                xp, g_mix, od_w, od_wf, od_bf, d_qg, d_kg, conv_w, e_ones, place, l, o, s_len, h_d, ffn1,
                stacked=d_kv, n_slots=n_odd)
            d_kv = (kt, vt)
            att = _flash(q, kaug, vaug, None, bp, s_len)
            xp = finish(xp, (yc, att), od_wo, o, p=pp)
            cv_p.append(tail[:, SUBLANES - (CONV_W - 1):, :])
            df_p.append(lf.reshape(bp, s_len, h_d))

            buf = state_c_conv[:, o]
            yc, q, k, v, lf, pre = _odd_in_sample(
                xs, g_mix, od_w, od_wf, od_bf, d_qg, d_kg, conv_w, e_ones,
                buf[:, 0], buf[:, 1], l, o)
            r3 = lambda a: a.reshape(bs, 1, -1)
            lf = lf[:, :h_d]
            att = _paged_attn(page_table, r3(q), r3(k), r3(v), ck_d, cv_d, o,
                              lf_new=lf.reshape(bs, h_d, 1), cache_f=cf_d).reshape(ts, d_d)
            xs = finish(xs, (yc, att), od_wo, o, p=ps)
            cv_s.append(jnp.stack([buf[:, 1], pre], axis=1))
            dk_s.append(k.reshape(bs, t_len, h_d, hd))
            dv_s.append(v.reshape(bs, t_len, h_d, hd))
            df_s.append(lf.reshape(bs, t_len, h_d))

    st = lambda rows: jnp.stack(rows, axis=1)
    seq_major = lambda a, heads: jnp.transpose(a.reshape(bp, a.shape[1], heads, hd, s_len), (0, 1, 4, 2, 3))
    return (xp.reshape(bp, s_len, d), xs.reshape(bs, t_len, d),
            seq_major(a_kv[0], h_a), seq_major(a_kv[1], h_a), st(ak_s), st(av_s), st(bv_s),
            st(cv_p), st(cv_s),
            seq_major(d_kv[0], h_d), seq_major(d_kv[1], h_d), st(df_p), st(dk_s), st(dv_s), st(df_s))
```

```python
import functools
import math

import numpy as np
import jax
import jax.numpy as jnp
from jax import lax
from jax.experimental import pallas as pl
from jax.experimental.pallas import tpu as pltpu

F32 = jnp.float32
BF16 = jnp.bfloat16

HEAD_DIM = 64
MOBA_BLOCK = 256
MOBA_TOPK = 3
GMLP_CHUNK = 128
CONV_W = 3
RMS_EPS = 1e-6
NEG_INF = -1e30
ATTN_SCALE = 1.0 / math.sqrt(HEAD_DIM)
LOG2E = math.log2(math.e)

LANES = 128
SUBLANES = 8
MXU_WIDTH = 256
HEADS_PER_TILE = LANES // HEAD_DIM
DECAY_LANE0 = 96
DECAY_TERMS = 3
PAIR_ROWS = LANES + 16
VMEM_LIMIT = 56 * 1024 * 1024
TOKEN_TILES = (512, 256, 128, 64, 32, 16, 8)
HIGHEST = lax.Precision.HIGHEST


def _params(*sem):
    return pltpu.CompilerParams(dimension_semantics=sem, vmem_limit_bytes=VMEM_LIMIT)


def _pick(n, options):
    for t in options:
        if n % t == 0:
            return t
    raise ValueError(f"no tile in {options} divides {n}")


def _rms(x, g):
    return x * lax.rsqrt(jnp.mean(x * x, axis=-1, keepdims=True) + RMS_EPS) * g


def _dot(a, b):
    return jnp.dot(a, b, preferred_element_type=F32)


def _dot_t(a, b, precision=None):
    return lax.dot_general(a, b, (((1,), (1,)), ((), ())), precision=precision,
                           preferred_element_type=F32)


def _split3(x):
    hi = x.astype(BF16)
    r = x - hi.astype(F32)
    mid = r.astype(BF16)
    lo = (r - mid.astype(F32)).astype(BF16)
    return hi, mid, lo


def _tree(op, xs):
    xs = list(xs)
    while len(xs) > 1:
        xs = [op(xs[i], xs[i + 1]) if i + 1 < len(xs) else xs[i] for i in range(0, len(xs), 2)]
    return xs[0]


def _log_sigmoid(x):
    return jnp.minimum(x, 0.0) - jnp.log1p(jnp.exp(-jnp.abs(x)))


def _head_norm(t, e, gain):
    ss = _dot((t * t).astype(BF16), e)
    return t * lax.rsqrt(ss * (1.0 / HEAD_DIM) + RMS_EPS) * gain


def _swiglu(x, g, wi_ref, wo_ref, chunks):
    f = wo_ref.shape[0]
    hn = _rms(x, g).astype(BF16)

    def gate_up(c0, c1):
        return _dot(hn, wi_ref[:, c0:c1]), _dot(hn, wi_ref[:, f + c0:f + c1])

    nxt = gate_up(*chunks[0])
    acc = None
    for i, (c0, c1) in enumerate(chunks):
        a, b = nxt
        if i + 1 < len(chunks):
            nxt = gate_up(*chunks[i + 1])
        part = _dot((jax.nn.silu(a) * b).astype(BF16), wo_ref[c0:c1, :])
        acc = part if acc is None else acc + part
    return acc


def _ffn_chunks(f):
    step = 2 * MXU_WIDTH
    chunks = tuple((c0, min(c0 + step, f)) for c0 in range(0, f, step))
    assert all((c1 - c0) % MXU_WIDTH == 0 for c0, c1 in chunks)
    return chunks


RESIDENT = dict(pipeline_mode=pl.Buffered(1))


def _ffn_body(x_ref, g_ref, wi_ref, wo_ref, o_ref, *, chunks):
    x = x_ref[...]
    o_ref[...] = x + 0.5 * _swiglu(x, g_ref[...], wi_ref, wo_ref, chunks)


def _ffn_half(x, g, wi, wo, layer):
    t, d = x.shape
    f = wo.shape[1]
    tm = _pick(t, TOKEN_TILES)
    return pl.pallas_call(
        functools.partial(_ffn_body, chunks=_ffn_chunks(f)),
        grid=(t // tm,),
        in_specs=[
            pl.BlockSpec((tm, d), lambda i: (i, 0)),
            pl.BlockSpec((None, 1, d), lambda i: (layer, 0, 0)),
            pl.BlockSpec((None, d, 2 * f), lambda i: (layer, 0, 0), **RESIDENT),
            pl.BlockSpec((None, f, d), lambda i: (layer, 0, 0), **RESIDENT),
        ],
        out_specs=pl.BlockSpec((tm, d), lambda i: (i, 0)),
        out_shape=jax.ShapeDtypeStruct((t, d), F32),
        compiler_params=_params("parallel"),
        name="ffn_half",
    )(x, g, wi, wo)


def _tail_body(*refs, chunks, gated):
    if gated:
        x_ref, a_ref, u_ref, vg_ref, wv_ref, bv_ref = refs[:6]
        b = u_ref[...] * (wv_ref[...] * vg_ref[...] + bv_ref[...])
        rest = refs[6:]
    else:
        x_ref, a_ref, b_ref = refs[:3]
        b = b_ref[...]
        rest = refs[3:]
    wm_ref, g2_ref, wi_ref, wo_ref, p_ref, gp_ref, wg_ref, wp_ref, o_ref = rest
    half = a_ref.shape[-1]
    proj = _dot(p_ref[...].astype(BF16), wp_ref[...])
    x = (x_ref[...] + _dot(a_ref[...].astype(BF16), wm_ref[0:half, :])
         + _dot(b.astype(BF16), wm_ref[half:2 * half, :]))
    x = x + 0.5 * _swiglu(x, g2_ref[...], wi_ref, wo_ref, chunks)
    gate = jax.nn.sigmoid(_dot(_rms(x, gp_ref[...]).astype(BF16), wg_ref[...]))
    o_ref[...] = x + gate * proj


def _layer_tail(x, parts, wm, mixer, g2, wi, wo, p, gp, wg, wp, layer, gate_vecs=None):
    t, d = x.shape
    half = parts[0].shape[-1]
    f = wo.shape[1]
    pdim = p.shape[-1]
    tm = _pick(t, TOKEN_TILES)
    row = lambda width: pl.BlockSpec((tm, width), lambda i: (i, 0))
    vec = pl.BlockSpec((1, half), lambda i: (0, 0))
    norm = pl.BlockSpec((None, 1, d), lambda i: (layer, 0, 0))
    in_specs = [row(d)] + [row(half)] * len(parts)
    args = [x, *parts]
    if gate_vecs is not None:
        in_specs += [vec, vec]
        args += list(gate_vecs)
    in_specs += [pl.BlockSpec((None, 2 * half, d), lambda i: (mixer, 0, 0), **RESIDENT),
                 norm,
                 pl.BlockSpec((None, d, 2 * f), lambda i: (layer, 0, 0), **RESIDENT),
                 pl.BlockSpec((None, f, d), lambda i: (layer, 0, 0), **RESIDENT),
                 pl.BlockSpec((None, tm, pdim), lambda i: (layer, i, 0)),
                 norm,
                 pl.BlockSpec((None, d, d), lambda i: (layer, 0, 0), **RESIDENT),
                 pl.BlockSpec((None, pdim, d), lambda i: (layer, 0, 0), **RESIDENT)]
    args += [wm, g2, wi, wo, p, gp, wg, wp]
    return pl.pallas_call(
        functools.partial(_tail_body, chunks=_ffn_chunks(f), gated=gate_vecs is not None),
        grid=(t // tm,),
        in_specs=in_specs,
        out_specs=row(d),
        out_shape=jax.ShapeDtypeStruct((t, d), F32),
        compiler_params=_params("parallel"),
        name="layer_tail",
    )(*args)


def _gmlp_body(u_ref, vg_ref, ws_ref, bias_ref, o_ref):
    tm = u_ref.shape[0]
    c = GMLP_CHUNK
    r_i = lax.broadcasted_iota(jnp.int32, (c, c), 0)
    c_i = lax.broadcasted_iota(jnp.int32, (c, c), 1)
    tril = c_i <= r_i
    first_head = c_i < HEAD_DIM
    n_groups = ws_ref.shape[0]
    w = [jnp.where(tril, ws_ref[g], 0.0).astype(BF16) for g in range(n_groups)]
    bias = bias_ref[...]
    for ci in range(tm // c):
        rows = pl.ds(ci * c, c)
        vg = vg_ref[rows, :].astype(BF16)
        tiles = []
        for t in range(n_groups // HEADS_PER_TILE):
            vt = vg[:, t * LANES:(t + 1) * LANES]
            tiles.append(jnp.where(first_head, _dot(w[2 * t], vt), _dot(w[2 * t + 1], vt)))
        mixed = jnp.concatenate(tiles, axis=1) + bias
        o_ref[rows, :] = u_ref[rows, :] * mixed


def _gmlp_gate(u, vg, ws, bias, layer):
    t, width = u.shape
    tm = _pick(t, TOKEN_TILES)
    g, c = ws.shape[1], ws.shape[2]
    row = pl.BlockSpec((tm, width), lambda i: (i, 0))
    return pl.pallas_call(
        _gmlp_body,
        grid=(t // tm,),
        in_specs=[row, row,
                  pl.BlockSpec((None, g, c, c), lambda i: (layer, 0, 0, 0)),
                  pl.BlockSpec((None, c, width), lambda i: (layer, 0, 0))],
        out_specs=row,
        out_shape=jax.ShapeDtypeStruct((t, width), F32),
        compiler_params=_params("parallel"),
        name="gmlp_gate",
    )(u, vg, ws, bias)


def _block_onehot(tile_rows, first_pos, nb):
    r = lax.broadcasted_iota(jnp.int32, (tile_rows, LANES), 0)
    c = lax.broadcasted_iota(jnp.int32, (tile_rows, LANES), 1)
    blk = lax.div(first_pos + r, MOBA_BLOCK)
    return jnp.where((c == blk) & (c < nb), 1.0, 0.0)


def _store_value_tiles(vaug_ref, v_t):
    ones = jnp.ones((PAIR_ROWS - LANES, MOBA_BLOCK), BF16)
    for j in range(vaug_ref.shape[0]):
        for t in range(v_t.shape[0] // LANES):
            tile = v_t[t * LANES:(t + 1) * LANES, j * MOBA_BLOCK:(j + 1) * MOBA_BLOCK]
            vaug_ref[j, t * PAIR_ROWS:t * PAIR_ROWS + LANES, :] = tile.astype(BF16)
            vaug_ref[j, t * PAIR_ROWS + LANES:(t + 1) * PAIR_ROWS, :] = ones


def _even_in_body(x_ref, g_ref, w_ref, qg_ref, kg_ref, vn_ref, e_ref, *rest, prompt, seq_len, n_alias, chunks):
    x = x_ref[...]
    if prompt:
        g1_ref, wi_ref, wo_ref = rest[:3]
        rest = rest[3:]
        x = x + 0.5 * _swiglu(x, g1_ref[...], wi_ref, wo_ref, chunks)
        rest[-1][...] = x
    outs = rest[n_alias:]
    q_ref, k_ref, v_ref, u_ref, vg_ref = outs[:5]
    half = q_ref.shape[-1]
    h = _rms(x, g_ref[...]).astype(BF16)
    e = e_ref[...]
    part = lambda n: _dot(h, w_ref[:, n * half:(n + 1) * half])
    z_q, z_k = part(0), part(1)
    q_ref[...] = _head_norm(z_q, e, qg_ref[...])
    z_vg = part(4)
    k = _head_norm(z_k, e, kg_ref[...])
    z_u = part(3)
    vg_ref[...] = _head_norm(jax.nn.gelu(z_vg), e, vn_ref[...])
    v = part(2)
    u_ref[...] = jax.nn.gelu(z_u)
    if prompt:
        v_t = v.T
        k_ref[...] = k.T
        v_ref[...] = v_t
    else:
        k_ref[...] = k
        v_ref[...] = v
    if prompt:
        kaug_ref, vaug_ref, km_ref = outs[5:8]
        tm = x_ref.shape[0]
        first_pos = lax.rem(pl.program_id(0) * tm, seq_len)
        onehot = _block_onehot(tm, first_pos, seq_len // MOBA_BLOCK).astype(BF16)
        kb = k.astype(BF16)
        for t in range(half // LANES):
            kaug_ref[:, 2 * t * LANES:(2 * t + 1) * LANES] = kb[:, t * LANES:(t + 1) * LANES]
            kaug_ref[:, (2 * t + 1) * LANES:(2 * t + 2) * LANES] = onehot
        _store_value_tiles(vaug_ref, v_t)
        for b in range(tm // MOBA_BLOCK):
            km_ref[b:b + 1, :] = jnp.mean(k[b * MOBA_BLOCK:(b + 1) * MOBA_BLOCK, :], axis=0, keepdims=True)


def _value_tiles_out(t, tm, half):
    per_step = tm // MOBA_BLOCK
    rows = half // LANES * PAIR_ROWS
    spec = pl.BlockSpec((per_step, rows, MOBA_BLOCK), lambda i: (i, 0, 0))
    return spec, jax.ShapeDtypeStruct((t // MOBA_BLOCK, rows, MOBA_BLOCK), BF16)


def _stacked_kv(stacked, n_slots, batch, width, seq_len, slot, tm, first_alias_input):
    tiles_per_seq = seq_len // tm
    spec = pl.BlockSpec((None, None, width, tm), lambda i: (i // tiles_per_seq, slot, 0, i % tiles_per_seq))
    shape = jax.ShapeDtypeStruct((batch, n_slots, width, seq_len), F32)
    if stacked is None:
        return spec, shape, [], [], {}
    any_spec = pl.BlockSpec(memory_space=pl.ANY)
    return spec, shape, [any_spec, any_spec], list(stacked), {first_alias_input: 1, first_alias_input + 1: 2}


def _ffn_inputs(ffn, layer, d):
    g1, wi, wo = ffn
    f = wo.shape[1]
    specs = [pl.BlockSpec((None, 1, d), lambda i: (layer, 0, 0)),
             pl.BlockSpec((None, d, 2 * f), lambda i: (layer, 0, 0), **RESIDENT),
             pl.BlockSpec((None, f, d), lambda i: (layer, 0, 0), **RESIDENT)]
    return specs, [g1, wi, wo], _ffn_chunks(f)


def _even_in(x, g, w, qg, kg, vn, e, layer, ev, prompt, seq_len, stacked=None, n_slots=1, ffn=None):
    t, d = x.shape
    n_in = w.shape[-1]
    half = n_in // 5
    tm = _pick(t, (512, 256) if prompt else TOKEN_TILES)
    row = lambda width: pl.BlockSpec((tm, width), lambda i: (i, 0))
    vec = lambda arr: pl.BlockSpec((None, 1, arr.shape[-1]), lambda i: (ev, 0, 0))
    in_specs = [row(d),
                pl.BlockSpec((None, 1, d), lambda i: (layer, 0, 0)),
                pl.BlockSpec((None, d, n_in), lambda i: (ev, 0, 0), **RESIDENT),
                vec(qg), vec(kg), vec(vn),
                pl.BlockSpec(e.shape, lambda i: (0, 0), **RESIDENT)]
    args = [x, g, w, qg, kg, vn, e]
    out_specs = [row(half)] * 5
    out_shape = [jax.ShapeDtypeStruct((t, half), F32)] * 5
    aliases = {}
    chunks = None
    if prompt:
        ffn_specs, ffn_args, chunks = _ffn_inputs(ffn, layer, d)
        in_specs += ffn_specs
        args += ffn_args
        kv_spec, kv_shape, alias_specs, alias_args, aliases = _stacked_kv(
            stacked, n_slots, t // seq_len, half, seq_len, ev, tm, len(in_specs))
        in_specs += alias_specs
        args += alias_args
        nkm = tm // MOBA_BLOCK
        vaug_spec, vaug_shape = _value_tiles_out(t, tm, half)
        out_specs = [row(half), kv_spec, kv_spec, row(half), row(half),
                     row(2 * half), vaug_spec, pl.BlockSpec((None, nkm, half), lambda i: (i, 0, 0)), row(d)]
        out_shape = [out_shape[0], kv_shape, kv_shape, out_shape[0], out_shape[0],
                     jax.ShapeDtypeStruct((t, 2 * half), BF16), vaug_shape,
                     jax.ShapeDtypeStruct((t // tm, nkm, half), F32), jax.ShapeDtypeStruct((t, d), F32)]
    return pl.pallas_call(
        functools.partial(_even_in_body, prompt=prompt, seq_len=seq_len, n_alias=len(aliases), chunks=chunks),
        grid=(t // tm,),
        in_specs=in_specs,
        out_specs=out_specs,
        out_shape=out_shape,
        input_output_aliases=aliases,
        compiler_params=_params("parallel"),
        name="even_in",
    )(*args)


def _odd_in_prompt_body(x_ref, g_ref, w_ref, wf_ref, bf_ref, qg_ref, kg_ref, cw_ref, e_ref, p_ref,
                        g1_ref, wi_ref, wo_ref, *rest, seq_len, n_alias, chunks):
    (q_ref, k_ref, v_ref, yc_ref, lf_ref, kaug_ref, vaug_ref, tail_ref, x1_ref,
     pre_ref, carry_ref) = rest[n_alias:]
    tm = x_ref.shape[0]
    half = q_ref.shape[-1]
    n_heads = lf_ref.shape[-1]
    tiles_per_seq = seq_len // tm
    s_idx = lax.rem(pl.program_id(0), tiles_per_seq)

    @pl.when(s_idx == 0)
    def _():
        pre_ref[0:SUBLANES, :] = jnp.zeros((SUBLANES, half), F32)
        carry_ref[...] = jnp.zeros_like(carry_ref)

    @pl.when(s_idx != 0)
    def _():
        pre_ref[0:SUBLANES, :] = pre_ref[tm:tm + SUBLANES, :]

    x = x_ref[...]
    x = x + 0.5 * _swiglu(x, g1_ref[...], wi_ref, wo_ref, chunks)
    x1_ref[...] = x
    h = _rms(x, g_ref[...]).astype(BF16)
    e = e_ref[...]
    part = lambda n: _dot(h, w_ref[:, n * half:(n + 1) * half])
    f_logit = _dot(h, wf_ref[...])
    z_gc, z_hc = part(1), part(2)

    lane = lax.broadcasted_iota(jnp.int32, (tm, LANES), 1)
    logf = jnp.where(lane < n_heads, _log_sigmoid(f_logit + bf_ref[...]), 0.0)
    lf_ref[...] = logf[:, 0:n_heads]
    r_i = lax.broadcasted_iota(jnp.int32, (tm, tm), 0)
    c_i = lax.broadcasted_iota(jnp.int32, (tm, tm), 1)
    tril = jnp.where(c_i <= r_i, 1.0, 0.0).astype(BF16)
    hi, mid, lo = _split3(logf)
    cum = _dot(tril, hi) + _dot(tril, mid) + _dot(tril, lo) + carry_ref[...]
    carry_ref[...] = cum[tm - 1:tm, :]
    gb = part(0)

    pre = z_gc * z_hc
    pre_ref[SUBLANES:SUBLANES + tm, :] = pre
    conv = (cw_ref[0:1, :] * pre_ref[SUBLANES - 2:SUBLANES - 2 + tm, :]
            + cw_ref[1:2, :] * pre_ref[SUBLANES - 1:SUBLANES - 1 + tm, :]
            + cw_ref[2:3, :] * pre)
    z_q = part(3)
    yc_ref[...] = gb * conv
    tail_ref[...] = pre[tm - SUBLANES:tm, :]
    z_k = part(4)
    q_ref[...] = _head_norm(z_q, e, qg_ref[...])
    v = part(5)
    k = _head_norm(z_k, e, kg_ref[...])
    v_t = v.T
    k_ref[...] = k.T
    v_ref[...] = v_t
    _store_value_tiles(vaug_ref, v_t)

    dec = _dot(jnp.concatenate(_split3(cum * (-LOG2E)), axis=1), p_ref[...])
    onehot = _block_onehot(tm, s_idx * tm, seq_len // MOBA_BLOCK)
    kb = k.astype(BF16)
    for t in range(half // LANES):
        kaug_ref[:, 2 * t * LANES:(2 * t + 1) * LANES] = kb[:, t * LANES:(t + 1) * LANES]
        kaug_ref[:, (2 * t + 1) * LANES:(2 * t + 2) * LANES] = (
            dec[:, t * LANES:(t + 1) * LANES] + onehot).astype(BF16)


def _odd_in_prompt(x, g, w, wf, bf, qg, kg, cw, e, place, layer, od, seq_len, n_heads, ffn, stacked=None, n_slots=1):
    t, d = x.shape
    half = qg.shape[-1]
    tm = _pick(seq_len, (512, 256))
    row = lambda width: pl.BlockSpec((tm, width), lambda i: (i, 0))
    vec = lambda arr, **kw: pl.BlockSpec((None,) + arr.shape[1:], lambda i: (od, 0, 0), **kw)
    full = lambda arr: pl.BlockSpec(arr.shape, lambda i: (0, 0), **RESIDENT)
    tiles_per_seq = seq_len // tm
    ffn_specs, ffn_args, chunks = _ffn_inputs(ffn, layer, d)
    in_specs = [row(d),
                pl.BlockSpec((None, 1, d), lambda i: (layer, 0, 0)),
                vec(w, **RESIDENT), vec(wf, **RESIDENT), vec(bf), vec(qg), vec(kg), vec(cw), full(e), full(place)]
    in_specs += ffn_specs
    kv_spec, kv_shape, alias_specs, alias_args, aliases = _stacked_kv(
        stacked, n_slots, t // seq_len, half, seq_len, od, tm, len(in_specs))
    vaug_spec, vaug_shape = _value_tiles_out(t, tm, half)
    return pl.pallas_call(
        functools.partial(_odd_in_prompt_body, seq_len=seq_len, n_alias=len(aliases), chunks=chunks),
        grid=(t // tm,),
        in_specs=in_specs + alias_specs,
        out_specs=[row(half), kv_spec, kv_spec, row(half), row(n_heads), row(2 * half), vaug_spec,
                   pl.BlockSpec((None, SUBLANES, half), lambda i: (i // tiles_per_seq, 0, 0)), row(d)],
        out_shape=[jax.ShapeDtypeStruct((t, half), F32), kv_shape, kv_shape, jax.ShapeDtypeStruct((t, half), F32),
                   jax.ShapeDtypeStruct((t, n_heads), F32),
                   jax.ShapeDtypeStruct((t, 2 * half), BF16), vaug_shape,
                   jax.ShapeDtypeStruct((t // seq_len, SUBLANES, half), F32), jax.ShapeDtypeStruct((t, d), F32)],
        scratch_shapes=[pltpu.VMEM((tm + 2 * SUBLANES, half), F32), pltpu.VMEM((1, LANES), F32)],
        input_output_aliases=aliases,
        compiler_params=_params("arbitrary"),
        name="odd_in_prompt",
    )(x, g, w, wf, bf, qg, kg, cw, e, place, *ffn_args, *alias_args)


def _odd_in_sample_body(x_ref, g_ref, w_ref, wf_ref, bf_ref, qg_ref, kg_ref, cw_ref, e_ref,
                        b0_ref, b1_ref, yc_ref, q_ref, k_ref, v_ref, lf_ref, pre_ref):
    half = q_ref.shape[-1]
    h = _rms(x_ref[...], g_ref[...]).astype(BF16)
    z = _dot(h, w_ref[...])
    e = e_ref[...]
    pre = z[:, half:2 * half] * z[:, 2 * half:3 * half]
    conv = cw_ref[0:1, :] * b0_ref[...] + cw_ref[1:2, :] * b1_ref[...] + cw_ref[2:3, :] * pre
    yc_ref[...] = z[:, 0:half] * conv
    pre_ref[...] = pre
    q_ref[...] = _head_norm(z[:, 3 * half:4 * half], e, qg_ref[...])
    k_ref[...] = _head_norm(z[:, 4 * half:5 * half], e, kg_ref[...])
    v_ref[...] = z[:, 5 * half:6 * half]
    lf_ref[...] = _log_sigmoid(_dot(h, wf_ref[...]) + bf_ref[...])


def _odd_in_sample(x, g, w, wf, bf, qg, kg, cw, e, b0, b1, layer, od):
    t, d = x.shape
    half = qg.shape[-1]
    tm = _pick(t, TOKEN_TILES)
    row = lambda width: pl.BlockSpec((tm, width), lambda i: (i, 0))
    vec = lambda arr: pl.BlockSpec((None,) + arr.shape[1:], lambda i: (od, 0, 0))
    return pl.pallas_call(
        _odd_in_sample_body,
        grid=(t // tm,),
        in_specs=[row(d),
                  pl.BlockSpec((None, 1, d), lambda i: (layer, 0, 0)),
                  vec(w), vec(wf), vec(bf), vec(qg), vec(kg), vec(cw),
                  pl.BlockSpec(e.shape, lambda i: (0, 0)), row(half), row(half)],
        out_specs=[row(half)] * 4 + [row(LANES), row(half)],
        out_shape=[jax.ShapeDtypeStruct((t, half), F32)] * 4
        + [jax.ShapeDtypeStruct((t, LANES), F32), jax.ShapeDtypeStruct((t, half), F32)],
        compiler_params=_params("parallel"),
        name="odd_in_sample",
    )(x, g, w, wf, bf, qg, kg, cw, e, b0, b1)


def _flash_body(q_ref, kaug_ref, v_ref, *rest, moba, nb, tk, n_tiles):
    if moba:
        km_ref, o_ref, qa_ref = rest
    else:
        o_ref, qa_ref = rest
    tq = MOBA_BLOCK
    qi = pl.program_id(2)
    lane = lax.broadcasted_iota(jnp.int32, (tq, LANES), 1)
    is_blk = lane < nb
    past = lane < qi
    low = lane < HEAD_DIM
    nb_rows = -(-nb // SUBLANES) * SUBLANES
    blk_t = lax.broadcasted_iota(jnp.int32, (nb_rows, tq), 0)
    past_t = blk_t < qi
    blk_f = blk_t.astype(F32)
    own_rows = pl.ds(pl.multiple_of(qi * tq, tq), tq)
    r_i = lax.broadcasted_iota(jnp.int32, (tq, tq), 0)
    c_i = lax.broadcasted_iota(jnp.int32, (tq, tq), 1)
    causal = r_i <= c_i
    causal = jnp.concatenate([causal] * HEADS_PER_TILE, axis=1)
    chains = [(t, hh) for t in range(n_tiles) for hh in range(HEADS_PER_TILE)]

    q_heads = [jnp.where(low if hh == 0 else ~low, q_ref[:, t * LANES:(t + 1) * LANES], 0.0) for t, hh in chains]
    if moba:
        gates = [_dot_t(km_ref[0:nb_rows, t * LANES:(t + 1) * LANES], q_heads[c], precision=HIGHEST)
                 for c, (t, hh) in enumerate(chains)]
    qa_own = []
    for c, (t, hh) in enumerate(chains):
        qh = q_heads[c]
        if moba:
            gate = jnp.where(past_t, gates[c], NEG_INF)
            bias_t = jnp.full((nb_rows, tq), NEG_INF, F32)
            for _ in range(MOBA_TOPK):
                top = jnp.max(gate, axis=0, keepdims=True)
                idx = jnp.min(jnp.where(gate == top, blk_f, float(LANES)), axis=0, keepdims=True)
                pick = blk_f == idx
                bias_t = jnp.where(pick, jnp.where(past_t, 0.0, NEG_INF), bias_t)
                gate = jnp.where(pick, -jnp.inf, gate)
            bias = jnp.concatenate([bias_t, jnp.full((LANES - nb_rows, tq), NEG_INF, F32)], axis=0).T
            aux_own = jnp.zeros((tq, LANES), F32)
        else:
            bias = jnp.where(past, 0.0, NEG_INF)
            d0 = DECAY_LANE0 + DECAY_TERMS * hh
            aux_own = jnp.where((lane >= d0) & (lane < d0 + DECAY_TERMS), 1.0, 0.0)
        aux = jnp.where(is_blk, bias, aux_own)
        qb = (qh * (ATTN_SCALE * LOG2E)).astype(BF16)
        qa_ref[t, hh * tq:(hh + 1) * tq, :] = jnp.concatenate([qb, aux.astype(BF16)], axis=1)
        qa_own.append(jnp.concatenate([qb, aux_own.astype(BF16)], axis=1))

    own_scores = [_dot_t(kaug_ref[own_rows, 2 * t * LANES:(2 * t + 2) * LANES],
                         jnp.concatenate(qa_own[HEADS_PER_TILE * t:HEADS_PER_TILE * (t + 1)], axis=0))
                  for t in range(n_tiles)]
    init = []
    for t in range(n_tiles):
        s = jnp.where(causal, own_scores[t], NEG_INF)
        m = jnp.max(s, axis=0, keepdims=True)
        init.append((m, _dot(v_ref[qi, t * PAIR_ROWS:(t + 1) * PAIR_ROWS, :], jnp.exp2(s - m).astype(BF16))))

    blocks_per_step = tk // tq

    def step(n_blk, first_blk, carry):
        rows = pl.ds(pl.multiple_of(first_blk * tq, tq), n_blk * tq)
        scores = [_dot_t(kaug_ref[rows, 2 * t * LANES:(2 * t + 2) * LANES], qa_ref[t])
                  for t in range(n_tiles)]
        new = []
        for t in range(n_tiles):
            m, acc = carry[t]
            m_new = jnp.maximum(m, jnp.max(scores[t], axis=0, keepdims=True))
            p = jnp.exp2(scores[t] - m_new).astype(BF16)
            pv = _dot(v_ref[first_blk, t * PAIR_ROWS:(t + 1) * PAIR_ROWS, :], p[0:tq, :])
            for j in range(1, n_blk):
                pv = pv + _dot(v_ref[first_blk + j, t * PAIR_ROWS:(t + 1) * PAIR_ROWS, :], p[j * tq:(j + 1) * tq, :])
            new.append((m_new, jnp.exp2(m - m_new) * acc + pv))
        return tuple(new)

    n_full = lax.div(qi, blocks_per_step)
    carry = lax.fori_loop(0, n_full, lambda g, c: step(blocks_per_step, g * blocks_per_step, c), tuple(init))
    left = qi - n_full * blocks_per_step
    tails = [lambda c: c] + [functools.partial(step, s, n_full * blocks_per_step) for s in range(1, blocks_per_step)]
    final = lax.switch(left, tails, carry) if blocks_per_step > 1 else carry
    for t in range(n_tiles):
        _, acc = final[t]
        inv = 1.0 / acc[LANES:LANES + 1, :]
        out_t = jnp.concatenate([acc[0:HEAD_DIM, 0:tq] * inv[:, 0:tq],
                                 acc[HEAD_DIM:LANES, tq:2 * tq] * inv[:, tq:2 * tq]], axis=0)
        o_ref[:, t * LANES:(t + 1) * LANES] = out_t.T


def _flash(q, kaug, vaug, kmean, batch, seq_len):
    t, width = q.shape
    tq = MOBA_BLOCK
    nq = seq_len // tq
    nb = seq_len // MOBA_BLOCK
    assert nb <= DECAY_LANE0 and DECAY_LANE0 + HEADS_PER_TILE * DECAY_TERMS <= LANES
    tk = _pick(seq_len, (1024, 512, 256))
    n_tiles = width // LANES
    cols = n_tiles * LANES
    moba = kmean is not None
    in_specs = [pl.BlockSpec((tq, cols), lambda b, hp, i: (b * nq + i, hp)),
                pl.BlockSpec((seq_len, 2 * cols), lambda b, hp, i: (b, hp), pipeline_mode=pl.Buffered(1)),
                pl.BlockSpec((nq, n_tiles * PAIR_ROWS, tq), lambda b, hp, i: (b, hp, 0),
                             pipeline_mode=pl.Buffered(1))]
    args = [q, kaug, vaug]
    if moba:
        in_specs.append(pl.BlockSpec((LANES, cols), lambda b, hp, i: (b, hp)))
        args.append(kmean)
    return pl.pallas_call(
        functools.partial(_flash_body, moba=moba, nb=nb, tk=tk, n_tiles=n_tiles),
        grid=(batch, width // cols, nq),
        in_specs=in_specs,
        out_specs=pl.BlockSpec((tq, cols), lambda b, hp, i: (b * nq + i, hp)),
        out_shape=jax.ShapeDtypeStruct((t, width), F32),
        scratch_shapes=[pltpu.VMEM((n_tiles, HEADS_PER_TILE * tq, 2 * LANES), BF16)],
        compiler_params=_params("parallel", "parallel", "arbitrary"),
        name="moba_attn" if moba else "fox_attn",
    )(*args)


def _paged_body(pt_ref, q_ref, kn_ref, vn_ref, *rest, moba, n_pages, n_heads):
    del pt_ref
    if moba:
        k_refs, v_refs, (o_ref,) = rest[:n_pages], rest[n_pages:2 * n_pages], rest[2 * n_pages:]
    else:
        lfn_ref = rest[0]
        k_refs, v_refs = rest[1:1 + n_pages], rest[1 + n_pages:1 + 2 * n_pages]
        f_refs, (o_ref,) = rest[1 + 2 * n_pages:1 + 3 * n_pages], rest[1 + 3 * n_pages:]
    page = k_refs[0].shape[1]
    width = q_ref.shape[-1]
    assert n_heads == SUBLANES
    q = q_ref[...]
    h_i = lax.broadcasted_iota(jnp.int32, (n_heads, width), 0)
    w_i = lax.broadcasted_iota(jnp.int32, (n_heads, width), 1)
    own = lax.div(w_i, HEAD_DIM) == h_i
    q8 = jnp.where(own, q, 0.0)
    h2_i = lax.broadcasted_iota(jnp.int32, (2 * n_heads, width), 0)
    w2_i = lax.broadcasted_iota(jnp.int32, (2 * n_heads, width), 1)
    qs = jnp.where(lax.div(w2_i, HEAD_DIM) == lax.rem(h2_i, n_heads), q, 0.0) * ATTN_SCALE
    qs_hi = qs.astype(BF16).astype(F32)
    q16 = jnp.where(h2_i < n_heads, qs_hi, qs - qs_hi).astype(BF16)
    s = []
    for p in range(n_pages):
        s16 = _dot(q16, k_refs[p][...].astype(BF16))
        s.append(s16[0:n_heads, :] + s16[n_heads:2 * n_heads, :])

    if moba:
        pages_per_blk = MOBA_BLOCK // page
        n_blk = n_pages // pages_per_blk
        assert MOBA_TOPK <= n_blk
        gate = [jnp.sum(_tree(jnp.add, s[b * pages_per_blk:(b + 1) * pages_per_blk]), axis=-1, keepdims=True)
                for b in range(n_blk)]
        for b in range(n_blk):
            beaten_by = [jnp.where(gate[o] >= gate[b] if o < b else gate[o] > gate[b], 1.0, 0.0)
                         for o in range(n_blk) if o != b]
            on = _tree(jnp.add, beaten_by) < float(MOBA_TOPK)
            for p in range(b * pages_per_blk, (b + 1) * pages_per_blk):
                s[p] = jnp.where(on, s[p], NEG_INF)
    else:
        r_i = lax.broadcasted_iota(jnp.int32, (page, page), 0)
        c_i = lax.broadcasted_iota(jnp.int32, (page, page), 1)
        after_in_page = jnp.where(r_i > c_i, 1.0, 0.0).astype(BF16)
        f_pages = [f_refs[p][...] for p in range(n_pages)]
        terms = []
        for f in f_pages:
            hi = f.astype(BF16).astype(F32)
            mid = (f - hi).astype(BF16).astype(F32)
            terms += [hi, mid, f - hi - mid]
        r = _dot(jnp.concatenate(terms, axis=0).astype(BF16), after_in_page)
        within = [r[(3 * p) * n_heads:(3 * p + 1) * n_heads] + r[(3 * p + 1) * n_heads:(3 * p + 2) * n_heads]
                  + r[(3 * p + 2) * n_heads:(3 * p + 3) * n_heads] for p in range(n_pages)]
        totals = [jnp.sum(f, axis=-1, keepdims=True) for f in f_pages]
        after = lfn_ref[...]
        for p in reversed(range(n_pages)):
            s[p] = s[p] + (within[p] + after)
            after = after + totals[p]

    s_self = jnp.sum(q8 * kn_ref[...], axis=-1, keepdims=True) * ATTN_SCALE
    m = jnp.maximum(s_self, jnp.max(_tree(jnp.maximum, s), axis=-1, keepdims=True))
    p_self = jnp.exp(s_self - m)
    e = [jnp.exp(s[p] - m) for p in range(n_pages)]
    l = p_self + jnp.sum(_tree(jnp.add, e), axis=-1, keepdims=True)
    acc = None
    for p in range(n_pages):
        term = v_refs[p][...].reshape(n_heads, HEAD_DIM, page) * e[p][:, None, :]
        acc = term if acc is None else acc + term
    past = _dot_t(jnp.ones((SUBLANES, page), BF16), acc.reshape(width, page).astype(BF16))[0:1, :]
    per_lane = lambda col: jnp.sum(jnp.where(own, col, 0.0), axis=0, keepdims=True)
    o_ref[...] = (past + per_lane(p_self) * vn_ref[...]) / per_lane(l)


def _paged_attn(page_table, q, kn, vn, cache_k, cache_v, layer_idx, lf_new=None, cache_f=None):
    n, _, width = q.shape
    n_pages = page_table.shape[1]
    n_heads = width // HEAD_DIM
    moba = cache_f is None
    one = pl.BlockSpec((None, 1, width), lambda s, pt: (s, 0, 0))

    def paged(arr, p):
        return pl.BlockSpec((None, None) + arr.shape[2:], lambda s, pt: (pt[s, p], layer_idx, 0, 0))

    in_specs = [one, one, one]
    args = [q, kn, vn]
    if not moba:
        in_specs.append(pl.BlockSpec((None, n_heads, 1), lambda s, pt: (s, 0, 0)))
        args.append(lf_new)
    in_specs += [paged(cache_k, p) for p in range(n_pages)] + [paged(cache_v, p) for p in range(n_pages)]
    args += [cache_k] * n_pages + [cache_v] * n_pages
    if not moba:
        in_specs += [paged(cache_f, p) for p in range(n_pages)]
        args += [cache_f] * n_pages
    return pl.pallas_call(
        functools.partial(_paged_body, moba=moba, n_pages=n_pages, n_heads=n_heads),
        grid_spec=pltpu.PrefetchScalarGridSpec(
            num_scalar_prefetch=1,
            grid=(n,),
            in_specs=in_specs,
            out_specs=pl.BlockSpec((None, 1, width), lambda s, pt: (s, 0, 0)),
        ),
        out_shape=jax.ShapeDtypeStruct((n, 1, width), F32),
        compiler_params=_params("arbitrary"),
        name="moba_paged" if moba else "fox_paged",
    )(page_table, *args)


def _head_ones(width):
    i = np.arange(width) // HEAD_DIM
    return jnp.asarray(i[:, None] == i[None, :], dtype=BF16)


def _decay_placement(n_heads):
    place = np.zeros((DECAY_TERMS * LANES, n_heads // HEADS_PER_TILE * LANES), np.float32)
    for c in range(DECAY_TERMS):
        for h in range(n_heads):
            col = (h // HEADS_PER_TILE) * LANES + DECAY_LANE0 + DECAY_TERMS * (h % HEADS_PER_TILE) + c
            place[c * LANES + h, col] = 1.0
    return jnp.asarray(place, dtype=BF16)


def kernel(x_prompt, x_sample, cache_a_k, cache_a_v, cache_d_k, cache_d_v, cache_d_logf, state_c_conv, page_table, p_prompt, p_sample, norm_ffn1, ffn1_wi, ffn1_wo, norm_mix, ev_w_in, a_q_norm, a_k_norm, b_v_norm, b_w_s, b_bias, ev_w_out, od_w_in, od_b_f, c_conv_w, d_q_norm, d_k_norm, od_w_out, norm_ffn2, ffn2_wi, ffn2_wo, norm_ple, ple_w_gate, ple_w_proj):
    bp, s_len, d = x_prompt.shape
    bs, t_len, _ = x_sample.shape
    depth = norm_ffn1.shape[0]
    n_pool, _, page, h_a, hd = cache_a_k.shape
    h_d = cache_d_k.shape[3]
    n_pages = page_table.shape[1]
    d_a, d_d = h_a * hd, h_d * hd
    d_c = c_conv_w.shape[-1]
    assert hd == HEAD_DIM and t_len == 1 and s_len % MOBA_BLOCK == 0 and d_a == d_d == d_c
    assert (n_pages * page) % MOBA_BLOCK == 0 and b_w_s.shape[-1] == GMLP_CHUNK
    tp, ts = bp * s_len, bs * t_len
    nb = s_len // MOBA_BLOCK

    bf = lambda w: w.astype(BF16)
    vec = lambda g: g.reshape(g.shape[0], 1, -1)
    ffn1_wi, ffn1_wo, ffn2_wi, ffn2_wo = bf(ffn1_wi), bf(ffn1_wo), bf(ffn2_wi), bf(ffn2_wo)
    ple_wg, ple_wp = bf(ple_w_gate), bf(ple_w_proj)
    ev_w, ev_wo, od_wo = bf(ev_w_in), bf(ev_w_out), bf(od_w_out)
    n_main = 3 * d_c + 3 * d_d
    od_w = bf(od_w_in[:, :, :n_main])
    od_wf = od_w_in[:, :, n_main:]
    od_wf = bf(jnp.pad(od_wf, ((0, 0), (0, 0), (0, LANES - h_d))))
    od_bf = jnp.pad(od_b_f, ((0, 0), (0, LANES - h_d)))[:, None, :]
    g_ffn1, g_mix, g_ffn2, g_ple = vec(norm_ffn1), vec(norm_mix), vec(norm_ffn2), vec(norm_ple)
    a_qg = jnp.tile(a_q_norm, (1, h_a))[:, None, :]
    a_kg = jnp.tile(a_k_norm, (1, h_a))[:, None, :]
    b_vn = vec(b_v_norm)
    d_qg = jnp.tile(d_q_norm, (1, h_d))[:, None, :]
    d_kg = jnp.tile(d_k_norm, (1, h_d))[:, None, :]
    conv_w = jnp.pad(c_conv_w, ((0, 0), (0, SUBLANES - CONV_W), (0, 0)))
    gate_bias = jnp.repeat(jnp.swapaxes(b_bias, 1, 2), HEAD_DIM, axis=2)
    gate_w1 = jnp.repeat(b_w_s[:, :, 0, 0], HEAD_DIM, axis=1)[:, None, :]
    gate_b1 = jnp.repeat(b_bias[:, :, 0], HEAD_DIM, axis=1)[:, None, :]
    e_ones = _head_ones(d_a)
    place = _decay_placement(h_d)

    pos_minor = lambda c: jnp.transpose(c, (0, 1, 3, 4, 2)).reshape(n_pool, c.shape[1], -1, page)
    ck_a, cv_a, ck_d, cv_d = pos_minor(cache_a_k), pos_minor(cache_a_v), pos_minor(cache_d_k), pos_minor(cache_d_v)
    cf_d = jnp.swapaxes(cache_d_logf, 2, 3)
    pp = p_prompt.reshape(depth, tp, -1)
    ps = p_sample.reshape(depth, ts, -1)

    xp = x_prompt.reshape(tp, d)
    xs = x_sample.reshape(ts, d)
    n_even, n_odd = (depth + 1) // 2, depth // 2
    a_kv = d_kv = None
    ak_s, av_s, bv_s = [], [], []
    cv_p, cv_s = [], []
    df_p, dk_s, dv_s, df_s = [], [], [], []

    for l in range(depth):
        ffn1 = (g_ffn1, ffn1_wi, ffn1_wo)
        xs = _ffn_half(xs, *ffn1, l)
        finish = functools.partial(_layer_tail, g2=g_ffn2, wi=ffn2_wi, wo=ffn2_wo, gp=g_ple, wg=ple_wg, wp=ple_wp,
                                 layer=l)
        if l % 2 == 0:
            e = l // 2
            q, kt, vt, u, vg, kaug, vaug, km, xp = _even_in(
                xp, g_mix, ev_w, a_qg, a_kg, b_vn, e_ones, l, e, True, s_len, stacked=a_kv, n_slots=n_even,
                ffn=ffn1)
            a_kv = (kt, vt)
            km = jnp.pad(km.reshape(bp, nb, d_a), ((0, 0), (0, LANES - nb), (0, 0))).reshape(bp * LANES, d_a)
            att = _flash(q, kaug, vaug, km, bp, s_len)
            sg = _gmlp_gate(u, vg, b_w_s, gate_bias, e)
            xp = finish(xp, (att, sg), ev_wo, e, p=pp)

            q, k, v, u, vg = _even_in(xs, g_mix, ev_w, a_qg, a_kg, b_vn, e_ones, l, e, False, s_len)
            r3 = lambda a: a.reshape(bs, 1, -1)
            att = _paged_attn(page_table, r3(q), r3(k), r3(v), ck_a, cv_a, e).reshape(ts, d_a)
            xs = finish(xs, (att, u, vg), ev_wo, e, p=ps, gate_vecs=(gate_w1[e], gate_b1[e]))
            ak_s.append(k.reshape(bs, t_len, h_a, hd))
            av_s.append(v.reshape(bs, t_len, h_a, hd))
            bv_s.append(vg.reshape(bs, t_len, -1))
        else:
            o = l // 2
            q, kt, vt, yc, lf, kaug, vaug, tail, xp = _odd_in_prompt(
                xp, g_mix, od_w, od_wf, od_bf, d_qg, d_kg, conv_w, e_ones, place, l, o, s_len, h_d, ffn1,
                stacked=d_kv, n_slots=n_odd)
            d_kv = (kt, vt)
            att = _flash(q, kaug, vaug, None, bp, s_len)
            xp = finish(xp, (yc, att), od_wo, o, p=pp)
            cv_p.append(tail[:, SUBLANES - (CONV_W - 1):, :])
            df_p.append(lf.reshape(bp, s_len, h_d))

            buf = state_c_conv[:, o]
            yc, q, k, v, lf, pre = _odd_in_sample(
                xs, g_mix, od_w, od_wf, od_bf, d_qg, d_kg, conv_w, e_ones,
                buf[:, 0], buf[:, 1], l, o)
            r3 = lambda a: a.reshape(bs, 1, -1)
            lf = lf[:, :h_d]
            att = _paged_attn(page_table, r3(q), r3(k), r3(v), ck_d, cv_d, o,
                              lf_new=lf.reshape(bs, h_d, 1), cache_f=cf_d).reshape(ts, d_d)
            xs = finish(xs, (yc, att), od_wo, o, p=ps)
            cv_s.append(jnp.stack([buf[:, 1], pre], axis=1))
            dk_s.append(k.reshape(bs, t_len, h_d, hd))
            dv_s.append(v.reshape(bs, t_len, h_d, hd))
            df_s.append(lf.reshape(bs, t_len, h_d))

    st = lambda rows: jnp.stack(rows, axis=1)
    seq_major = lambda a, heads: jnp.transpose(a.reshape(bp, a.shape[1], heads, hd, s_len), (0, 1, 4, 2, 3))
    return (xp.reshape(bp, s_len, d), xs.reshape(bs, t_len, d),
            seq_major(a_kv[0], h_a), seq_major(a_kv[1], h_a), st(ak_s), st(av_s), st(bv_s),
            st(cv_p), st(cv_s),
            seq_major(d_kv[0], h_d), seq_major(d_kv[1], h_d), st(df_p), st(dk_s), st(dv_s), st(df_s))
```
